```python
import jax, jax.numpy as jnp
from jax import lax
import numpy as np

D_MODEL = 1024
BATCH = 8
SEQ = 2048
DEPTH = 2
DEC_BATCH = 128
DEC_SEQ = 4
PAST_LEN = 16384
PAGE_SIZE = 128

D_CONV = D_MODEL
CONV_WIDTH = 31
CONV_BUF = CONV_WIDTH - 1
M_HEADS = 4
D_M = 2 * D_MODEL
DV = D_M // M_HEADS
DK = DV // 2
D_QK = M_HEADS * DK
D_FF = 4 * D_MODEL
CHUNK = 128
EPS = 1e-6
FGATE_BIAS_INIT = 3.0
IN_SIZES = (D_CONV, D_CONV, D_QK, D_QK, D_M, D_M, M_HEADS, M_HEADS, D_MODEL, D_MODEL)
D_IN = sum(IN_SIZES)

kernel_name = "conformer_mlstm_gated_hybrid_step"


def rmsnorm(x, g):
    xf = x.astype(jnp.float32)
    y = xf * lax.rsqrt(jnp.mean(xf * xf, axis=-1, keepdims=True) + EPS)
    return (y * g.astype(jnp.float32)).astype(x.dtype)


def layernorm(x, g, b):
    xf = x.astype(jnp.float32)
    mu = jnp.mean(xf, axis=-1, keepdims=True)
    xc = xf - mu
    y = xc * lax.rsqrt(jnp.mean(xc * xc, axis=-1, keepdims=True) + EPS)
    return (y * g.astype(jnp.float32) + b.astype(jnp.float32)).astype(x.dtype)


def split_in(proj):
    idx = []
    acc = 0
    for s in IN_SIZES[:-1]:
        acc += s
        idx.append(acc)
    return jnp.split(proj, idx, axis=-1)


def mlstm_scan(q, k, v, ig, lf, C0, n0, m0):
    B, T = q.shape[0], q.shape[1]
    L = min(CHUNK, T)
    NC = T // L
    mask = jnp.tril(jnp.ones((L, L), dtype=bool))

    def chunks(a):
        return jnp.moveaxis(a.reshape((B, NC, L) + a.shape[2:]), 1, 0)

    def step(carry, xs):
        C, n, m = carry
        qc, kc, vc, ic, fc = xs
        b = jnp.cumsum(fc, axis=1)
        D = b[:, :, None, :] - b[:, None, :, :] + ic[:, None, :, :]
        D = jnp.where(mask[None, :, :, None], D, -jnp.inf)
        inter = b + m[:, None, :]
        m_t = jnp.maximum(inter, jnp.max(D, axis=2))
        w_intra = jnp.exp(D - m_t[:, :, None, :])
        w_inter = jnp.exp(inter - m_t)
        s = jnp.einsum('bthk,bshk->btsh', qc, kc) * w_intra
        num = jnp.einsum('btsh,bshv->bthv', s, vc) + w_inter[..., None] * jnp.einsum('bhvk,bthk->bthv', C, qc)
        den = jnp.sum(s, axis=2) + w_inter * jnp.einsum('bhk,bthk->bth', n, qc)
        h = num / jnp.maximum(jnp.abs(den), jnp.exp(-m_t))[..., None]
        bL = b[:, -1]
        dL = bL[:, None, :] - b + ic
        m_new = jnp.maximum(bL + m, jnp.max(dL, axis=1))
        wk = jnp.exp(dL - m_new[:, None, :])
        dec = jnp.exp(bL + m - m_new)
        C_new = dec[..., None, None] * C + jnp.einsum('bsh,bshv,bshk->bhvk', wk, vc, kc)
        n_new = dec[..., None] * n + jnp.einsum('bsh,bshk->bhk', wk, kc)
        return (C_new, n_new, m_new), h

    (C, n, m), h = lax.scan(step, (C0, n0, m0),
                            (chunks(q), chunks(k), chunks(v), chunks(ig), chunks(lf)))
    h = jnp.moveaxis(h, 0, 1).reshape(B, T, M_HEADS, DV)
    return h, C, n, m


def trunk_layer(x, conv_buf, C0, n0, m0, g_pre_mix, w_in, b_in, w_dw, b_dw, g_cln, b_cln,
                w_conv_out, g_mh, w_m_out, w_o, g_post_mix, g_pre_ff, w_ff1, b_ff1, w_ff2, b_ff2,
                g_post_ff):
    B, T, _ = x.shape
    f32 = jnp.float32
    h = rmsnorm(x, g_pre_mix)
    proj = jnp.einsum('btd,de->bte', h, w_in) + b_in
    u_a, u_g, q, k, v, o, ig, fg, gate_conv, gate_m = split_in(proj)

    u = u_a * jax.nn.sigmoid(u_g)
    full = jnp.concatenate([conv_buf.astype(u.dtype), u], axis=1)
    c = lax.conv_general_dilated(full, w_dw[:, None, :], window_strides=(1,), padding='VALID',
                                 dimension_numbers=('NWC', 'WIO', 'NWC'),
                                 feature_group_count=D_CONV) + b_dw
    c = jax.nn.silu(layernorm(c, g_cln, b_cln))
    y_conv = jnp.einsum('btc,cd->btd', c, w_conv_out)
    new_buf = full[:, -CONV_BUF:]

    qh = q.reshape(B, T, M_HEADS, DK).astype(f32) * (DK ** -0.5)
    kh = k.reshape(B, T, M_HEADS, DK).astype(f32)
    vh = v.reshape(B, T, M_HEADS, DV).astype(f32)
    hm, C, n, m = mlstm_scan(qh, kh, vh, ig.astype(f32), jax.nn.log_sigmoid(fg.astype(f32)),
                             C0.astype(f32), n0.astype(f32), m0.astype(f32))
    hm = hm * lax.rsqrt(jnp.mean(hm * hm, axis=-1, keepdims=True) + EPS) \
        * g_mh.reshape(M_HEADS, DV).astype(f32)
    hm = hm.reshape(B, T, D_M).astype(x.dtype) * jax.nn.sigmoid(o)
    y_m = jnp.einsum('btm,md->btd', hm, w_m_out)

    mix = jax.nn.sigmoid(gate_conv) * y_conv + jax.nn.sigmoid(gate_m) * y_m
    x = x + rmsnorm(jnp.einsum('btd,de->bte', mix, w_o), g_post_mix)

    hf = rmsnorm(x, g_pre_ff)
    ff = jnp.einsum('btf,fd->btd', jnp.square(jax.nn.relu(jnp.einsum('btd,df->btf', hf, w_ff1) + b_ff1)), w_ff2) + b_ff2
    x = x + rmsnorm(ff, g_post_ff)
    return x, new_buf, C, n, m


def setup_inputs(seed: int = 0) -> dict:
    key = jax.random.key(seed)
    ks = jax.random.split(key, 32)
    nrm = jax.random.normal
    f32 = jnp.float32
    x_prompt = nrm(ks[0], (BATCH, SEQ, D_MODEL), f32)
    x_sample = nrm(ks[1], (DEC_BATCH, DEC_SEQ, D_MODEL), f32)
    state_conv = nrm(ks[2], (DEPTH, DEC_BATCH, CONV_BUF, D_CONV), f32)
    state_C = 0.1 * nrm(ks[3], (DEPTH, DEC_BATCH, M_HEADS, DV, DK), f32)
    state_n = 0.5 * nrm(ks[4], (DEPTH, DEC_BATCH, M_HEADS, DK), f32)
    state_m = nrm(ks[5], (DEPTH, DEC_BATCH, M_HEADS), f32)

    def gain(k, n):
        return 1.0 + 0.05 * nrm(k, (DEPTH, n), f32)

    f_start = sum(IN_SIZES[:7])
    b_in = 0.02 * nrm(ks[8], (DEPTH, D_IN), f32)
    b_in = b_in.at[:, f_start:f_start + M_HEADS].add(FGATE_BIAS_INIT)
    return {
        "x_prompt": x_prompt,
        "x_sample": x_sample,
        "state_conv": state_conv,
        "state_C": state_C,
        "state_n": state_n,
        "state_m": state_m,
        "g_pre_mix": gain(ks[6], D_MODEL),
        "w_in": nrm(ks[7], (DEPTH, D_MODEL, D_IN), f32) * D_MODEL ** -0.5,
        "b_in": b_in,
        "w_dw": nrm(ks[9], (DEPTH, CONV_WIDTH, D_CONV), f32) * CONV_WIDTH ** -0.5,
        "b_dw": 0.02 * nrm(ks[10], (DEPTH, D_CONV), f32),
        "g_cln": gain(ks[11], D_CONV),
        "b_cln": 0.02 * nrm(ks[12], (DEPTH, D_CONV), f32),
        "w_conv_out": nrm(ks[13], (DEPTH, D_CONV, D_MODEL), f32) * D_CONV ** -0.5,
        "g_mh": gain(ks[14], D_M),
        "w_m_out": nrm(ks[15], (DEPTH, D_M, D_MODEL), f32) * D_M ** -0.5,
        "w_o": nrm(ks[16], (DEPTH, D_MODEL, D_MODEL), f32) * D_MODEL ** -0.5,
        "g_post_mix": gain(ks[17], D_MODEL),
        "g_pre_ff": gain(ks[18], D_MODEL),
        "w_ff1": nrm(ks[19], (DEPTH, D_MODEL, D_FF), f32) * D_MODEL ** -0.5,
        "b_ff1": 0.02 * nrm(ks[20], (DEPTH, D_FF), f32),
        "w_ff2": nrm(ks[21], (DEPTH, D_FF, D_MODEL), f32) * D_FF ** -0.5,
        "b_ff2": 0.02 * nrm(ks[22], (DEPTH, D_MODEL), f32),
        "g_post_ff": gain(ks[23], D_MODEL),
    }


def reference(x_prompt, x_sample, state_conv, state_C, state_n, state_m,
              g_pre_mix, w_in, b_in, w_dw, b_dw, g_cln, b_cln, w_conv_out, g_mh, w_m_out,
              w_o, g_post_mix, g_pre_ff, w_ff1, b_ff1, w_ff2, b_ff2, g_post_ff):
    f32 = jnp.float32
    Bp = x_prompt.shape[0]
    yp, ys = x_prompt, x_sample
    cp_l, Cp_l, np_l, mp_l = [], [], [], []
    cs_l, Cs_l, ns_l, ms_l = [], [], [], []
    for d in range(DEPTH):
        prm = (g_pre_mix[d], w_in[d], b_in[d], w_dw[d], b_dw[d], g_cln[d], b_cln[d],
               w_conv_out[d], g_mh[d], w_m_out[d], w_o[d], g_post_mix[d], g_pre_ff[d],
               w_ff1[d], b_ff1[d], w_ff2[d], b_ff2[d], g_post_ff[d])
        buf0 = jnp.zeros((Bp, CONV_BUF, D_CONV), x_prompt.dtype)
        C0 = jnp.zeros((Bp, M_HEADS, DV, DK), f32)
        n0 = jnp.zeros((Bp, M_HEADS, DK), f32)
        m0 = jnp.zeros((Bp, M_HEADS), f32)
        yp, bp, Cp, npn, mp = trunk_layer(yp, buf0, C0, n0, m0, *prm)
        cp_l.append(bp); Cp_l.append(Cp); np_l.append(npn); mp_l.append(mp)
        ys, bs, Cs, nsn, ms = trunk_layer(ys, state_conv[d], state_C[d], state_n[d], state_m[d], *prm)
        cs_l.append(bs.astype(state_conv.dtype)); Cs_l.append(Cs.astype(state_C.dtype))
        ns_l.append(nsn.astype(state_n.dtype)); ms_l.append(ms.astype(state_m.dtype))
    return (yp, ys,
            jnp.stack(cp_l), jnp.stack(Cp_l), jnp.stack(np_l), jnp.stack(mp_l),
            jnp.stack(cs_l), jnp.stack(Cs_l), jnp.stack(ns_l), jnp.stack(ms_l))
```

```python
import functools

import jax
import jax.numpy as jnp
from jax import lax
from jax.experimental import pallas as pl
from jax.experimental.pallas import tpu as pltpu

F32 = jnp.float32
BF16 = jnp.bfloat16

D_MODEL = 1024
M_HEADS = 4
DK = 256
DV = 512
D_QK = M_HEADS * DK
D_M = M_HEADS * DV
D_FF = 4 * D_MODEL
CONV_WIDTH = 31
CONV_BUF = CONV_WIDTH - 1
CHUNK = 128
EPS = 1e-6
NEG = -1e30
LANES = 128
GATE_W = 2 * LANES
HALO = 32
CONV_ROWS = 16
SEQ_PER_STEP = 4

VMEM_LIMIT = 56 * 1024 * 1024


def _cparams(n_axes):
    return pltpu.CompilerParams(dimension_semantics=("arbitrary",) * n_axes,
                                vmem_limit_bytes=VMEM_LIMIT)


def _resident(shape):
    nd = len(shape)
    return pl.BlockSpec(shape, lambda *_: (0,) * nd, pipeline_mode=pl.Buffered(1))


def _rows(tm, width):
    return pl.BlockSpec((tm, width), lambda i: (i, 0))


def _rmsnorm(x, g):
    return x * lax.rsqrt(jnp.mean(x * x, axis=-1, keepdims=True) + EPS) * g


def _sigmoid(x):
    return 1.0 / (1.0 + jnp.exp(-x))


def _log_sigmoid(x):
    return jnp.minimum(x, 0.0) - jnp.log(1.0 + jnp.exp(-jnp.abs(x)))


def _dot(a, b):
    return jnp.dot(a, b, preferred_element_type=F32)


def _dot_nt(a, b):
    return lax.dot_general(a, b, (((1,), (1,)), ((), ())), preferred_element_type=F32)


def _dot_tn(a, b):
    return lax.dot_general(a, b, (((0,), (0,)), ((), ())), preferred_element_type=F32)


def _inproj_body(x_ref, g_ref, w_ref, b_ref, wg_ref, bg_ref,
                 u_ref, q_ref, k_ref, v_ref, so_ref, gc_ref, gm_ref, gi_ref, gf_ref):
    hb = _rmsnorm(x_ref[...], g_ref[...]).astype(BF16)

    def seg(lo, hi):
        return _dot(hb, w_ref[:, lo:hi]) + b_ref[:, lo:hi]

    o = 0
    u_ref[...] = seg(o, o + D_MODEL) * _sigmoid(seg(o + D_MODEL, o + 2 * D_MODEL))
    o += 2 * D_MODEL
    q_ref[...] = (seg(o, o + D_QK) * (DK ** -0.5)).astype(BF16)
    o += D_QK
    k_ref[...] = seg(o, o + D_QK).astype(BF16)
    o += D_QK
    v_ref[...] = seg(o, o + D_M).astype(BF16)
    o += D_M
    so_ref[...] = _sigmoid(seg(o, o + D_M)).astype(BF16)
    o += D_M
    gc_ref[...] = _sigmoid(seg(o, o + D_MODEL)).astype(BF16)
    o += D_MODEL
    gm_ref[...] = _sigmoid(seg(o, o + D_MODEL)).astype(BF16)
    g = _dot(hb, wg_ref[...]) + bg_ref[...]
    gi_ref[...] = g[:, :LANES]
    gf_ref[...] = _log_sigmoid(g[:, LANES:])


def _inproj(x, g, w, b, wg, bg, tm):
    n = x.shape[0]
    wcols = w.shape[1]
    widths = (D_MODEL, D_QK, D_QK, D_M, D_M, D_MODEL, D_MODEL, LANES, LANES)
    dtypes = (F32, BF16, BF16, BF16, BF16, BF16, BF16, F32, F32)
    return pl.pallas_call(
        _inproj_body,
        grid=(n // tm,),
        in_specs=[_rows(tm, D_MODEL), _resident((1, D_MODEL)), _resident((D_MODEL, wcols)),
                  _resident((1, wcols)), _resident((D_MODEL, GATE_W)), _resident((1, GATE_W))],
        out_specs=[_rows(tm, wd) for wd in widths],
        out_shape=[jax.ShapeDtypeStruct((n, wd), dt) for wd, dt in zip(widths, dtypes)],
        compiler_params=_cparams(1),
        name="inproj",
    )(x, g, w, b, wg, bg)


def _ln_silu(c, g, b):
    mu = jnp.mean(c, axis=-1, keepdims=True)
    xc = c - mu
    y = xc * lax.rsqrt(jnp.mean(xc * xc, axis=-1, keepdims=True) + EPS) * g + b
    return y * _sigmoid(y)


def _conv_prompt_body(u_ref, wdw_ref, bdw_ref, gln_ref, bln_ref, c_ref, ubuf):
    tc = u_ref.shape[0]

    @pl.when(pl.program_id(1) == 0)
    def _():
        ubuf[0:HALO, :] = jnp.zeros((HALO, D_MODEL), F32)

    ubuf[HALO:HALO + tc, :] = u_ref[...]

    for base in range(0, tc, CONV_ROWS):
        acc = jnp.broadcast_to(bdw_ref[...], (CONV_ROWS, D_MODEL))
        for j in range(CONV_WIDTH):
            lo = base + HALO - CONV_BUF + j
            acc = acc + ubuf[lo:lo + CONV_ROWS, :] * wdw_ref[j:j + 1, :]
        c_ref[base:base + CONV_ROWS, :] = _ln_silu(acc, gln_ref[...], bln_ref[...]).astype(c_ref.dtype)
    ubuf[0:HALO, :] = ubuf[tc:tc + HALO, :]


def _conv_prompt(u, batch, seq, wdw, bdw, gln, bln, tc):
    nt = seq // tc
    return pl.pallas_call(
        _conv_prompt_body,
        grid=(batch, nt),
        in_specs=[pl.BlockSpec((tc, D_MODEL), lambda b, i: (b * nt + i, 0)),
                  _resident((HALO, D_MODEL)), _resident((1, D_MODEL)),
                  _resident((1, D_MODEL)), _resident((1, D_MODEL))],
        out_specs=pl.BlockSpec((tc, D_MODEL), lambda b, i: (b * nt + i, 0)),
        out_shape=jax.ShapeDtypeStruct((batch * seq, D_MODEL), BF16),
        scratch_shapes=[pltpu.VMEM((HALO + tc, D_MODEL), F32)],
        compiler_params=_cparams(2),
        name="conv_prompt",
    )(u, wdw, bdw, gln, bln)


def _conv_sample_body(st_ref, u_ref, wdw_ref, bdw_ref, gln_ref, bln_ref, c_ref, fbuf):
    nseq, steps = st_ref.shape[0], u_ref.shape[0] // st_ref.shape[0]

    @pl.when(pl.program_id(0) == 0)
    def _():
        fbuf[...] = jnp.zeros(fbuf.shape, F32)

    for s in range(nseq):
        fbuf[0:CONV_BUF, :] = st_ref[s]
        fbuf[CONV_BUF:CONV_BUF + steps, :] = u_ref[s * steps:(s + 1) * steps, :]
        acc = jnp.broadcast_to(bdw_ref[...], (8, D_MODEL))
        for j in range(CONV_WIDTH):
            acc = acc + fbuf[j:j + 8, :] * wdw_ref[j:j + 1, :]
        c_ref[s * steps:(s + 1) * steps, :] = _ln_silu(acc, gln_ref[...], bln_ref[...])[:steps, :]


def _conv_sample(state, u, wdw, bdw, gln, bln, nseq):
    batch = state.shape[0]
    steps = u.shape[0] // batch
    return pl.pallas_call(
        _conv_sample_body,
        grid=(batch // nseq,),
        in_specs=[pl.BlockSpec((nseq, CONV_BUF, D_MODEL), lambda i: (i, 0, 0)),
                  _rows(nseq * steps, D_MODEL),
                  _resident((HALO, D_MODEL)), _resident((1, D_MODEL)),
                  _resident((1, D_MODEL)), _resident((1, D_MODEL))],
        out_specs=_rows(nseq * steps, D_MODEL),
        out_shape=jax.ShapeDtypeStruct(u.shape, F32),
        scratch_shapes=[pltpu.VMEM((CONV_BUF + 8 + 2, D_MODEL), F32)],
        compiler_params=_cparams(1),
        name="conv_sample",
    )(state, u, wdw, bdw, gln, bln)


def _col(x, h):
    return x[:, h:h + 1]


def _intra_head(qh, kh, vh, b_col, r_row, inter_col, mask):
    s = _dot_nt(qh, kh)
    dm = jnp.where(mask, b_col + r_row, NEG)
    m_t = jnp.maximum(inter_col, jnp.max(dm, axis=1, keepdims=True))
    sw = s * jnp.exp(dm - m_t)
    num = _dot(sw.astype(BF16), vh)
    den = jnp.sum(sw, axis=1, keepdims=True)
    return num, den, m_t, jnp.exp(inter_col - m_t)


def _head_out(num, den, m_t, gmh, so):
    hh = num * (1.0 / jnp.maximum(jnp.abs(den), jnp.exp(-m_t)))
    hn = hh * lax.rsqrt(jnp.mean(hh * hh, axis=-1, keepdims=True) + EPS) * gmh
    return (hn * so.astype(F32)).astype(BF16)


def _mlstm_prompt_body(q_ref, k_ref, v_ref, so_ref, gi_ref, gf_ref, gmh_ref,
                       hmo_ref, c_ref, n_ref, m_ref):
    L = q_ref.shape[0]

    @pl.when(pl.program_id(1) == 0)
    def _():
        c_ref[...] = jnp.zeros(c_ref.shape, F32)
        n_ref[...] = jnp.zeros(n_ref.shape, F32)
        m_ref[...] = jnp.zeros(m_ref.shape, F32)

    row = lax.broadcasted_iota(jnp.int32, (L, LANES), 0)
    gi = gi_ref[...]
    b = gf_ref[...]
    sh = 1
    while sh < L:
        b = b + jnp.where(row >= sh, pltpu.roll(b, sh, 0), 0.0)
        sh *= 2
    r = gi - b
    r_t = r.T
    m_prev = m_ref[0]
    b_last = b[L - 1:L, :]
    d_last = b_last + r
    m_new = jnp.maximum(b_last + m_prev, jnp.max(d_last, axis=0, keepdims=True))
    wk = jnp.exp(d_last - m_new)
    dec = jnp.exp(b_last + m_prev - m_new)
    inter = b + m_prev
    tri = lax.broadcasted_iota(jnp.int32, (L, L), 1) <= lax.broadcasted_iota(jnp.int32, (L, L), 0)

    for h in range(M_HEADS):
        qh = q_ref[:, h * DK:(h + 1) * DK]
        kh = k_ref[:, h * DK:(h + 1) * DK]
        vh = v_ref[:, h * DV:(h + 1) * DV]
        num, den, m_t, w_int = _intra_head(qh, kh, vh, _col(b, h), r_t[h:h + 1, :], _col(inter, h), tri)
        c_old = c_ref[0, h]
        n_old = n_ref[0, h:h + 1, :]
        num = num + w_int * _dot_nt(qh, c_old.astype(BF16))
        den = den + w_int * jnp.sum(qh.astype(F32) * n_old, axis=1, keepdims=True)
        hmo_ref[:, h * DV:(h + 1) * DV] = _head_out(
            num, den, m_t, gmh_ref[:, h * DV:(h + 1) * DV], so_ref[:, h * DV:(h + 1) * DV])
        wk_h = _col(wk, h)
        dec_h = _col(dec, h)
        vw = (vh.astype(F32) * wk_h).astype(BF16)
        c_ref[0, h] = dec_h * c_old + _dot_tn(vw, kh)
        n_ref[0, h:h + 1, :] = dec_h * n_old + jnp.sum(kh.astype(F32) * wk_h, axis=0, keepdims=True)
    m_ref[0] = m_new


def _mlstm_prompt(q, k, v, so, gi, gf, gmh, batch, seq):
    nc = seq // CHUNK
    n = batch * seq

    def tok(width):
        return pl.BlockSpec((CHUNK, width), lambda b, c: (b * nc + c, 0))

    return pl.pallas_call(
        _mlstm_prompt_body,
        grid=(batch, nc),
        in_specs=[tok(D_QK), tok(D_QK), tok(D_M), tok(D_M), tok(LANES), tok(LANES), _resident((1, D_M))],
        out_specs=[tok(D_M),
                   pl.BlockSpec((1, M_HEADS, DV, DK), lambda b, c: (b, 0, 0, 0)),
                   pl.BlockSpec((1, M_HEADS, DK), lambda b, c: (b, 0, 0)),
                   pl.BlockSpec((1, 1, LANES), lambda b, c: (b, 0, 0))],
        out_shape=[jax.ShapeDtypeStruct((n, D_M), BF16),
                   jax.ShapeDtypeStruct((batch, M_HEADS, DV, DK), F32),
                   jax.ShapeDtypeStruct((batch, M_HEADS, DK), F32),
                   jax.ShapeDtypeStruct((batch, 1, LANES), F32)],
        compiler_params=_cparams(2),
        name="mlstm_prompt",
    )(q, k, v, so, gi, gf, gmh)


def _mlstm_sample_intra_body(steps, q_ref, k_ref, v_ref, gi_ref, gf_ref, mtok_ref,
                             num_ref, den_ref, mt_ref, wi_ref, wk_ref, wkt_ref, dec_ref, mnew_ref):
    L = q_ref.shape[0]
    row = lax.broadcasted_iota(jnp.int32, (L, LANES), 0)
    pos = row & (steps - 1)

    def down(x, k):
        return pltpu.roll(x, k, 0)

    def up(x, k):
        return pltpu.roll(x, L - k, 0)

    gi = gi_ref[...]
    b = gf_ref[...]
    sh = 1
    while sh < steps:
        b = b + jnp.where(pos >= sh, down(b, sh), 0.0)
        sh *= 2
    r = gi - b
    r_t = r.T
    m_prev = mtok_ref[...]
    b_last = b
    for k in range(1, steps):
        b_last = jnp.where(pos == steps - 1 - k, up(b, k), b_last)
    d_last = b_last + r
    d_max = d_last
    sh = 1
    while sh < steps:
        d_max = jnp.maximum(d_max, jnp.where((pos & sh) == 0, up(d_max, sh), down(d_max, sh)))
        sh *= 2
    m_new = jnp.maximum(b_last + m_prev, d_max)
    wk = jnp.exp(d_last - m_new)
    inter = b + m_prev
    ti = lax.broadcasted_iota(jnp.int32, (L, L), 0)
    si = lax.broadcasted_iota(jnp.int32, (L, L), 1)
    mask = (si <= ti) & ((si & -steps) == (ti & -steps))
    lane = lax.broadcasted_iota(jnp.int32, (L, LANES), 1)

    den_all = jnp.zeros((L, LANES), F32)
    mt_all = jnp.zeros((L, LANES), F32)
    wi_all = jnp.zeros((L, LANES), F32)
    for h in range(M_HEADS):
        num, den, m_t, w_int = _intra_head(
            q_ref[:, h * DK:(h + 1) * DK], k_ref[:, h * DK:(h + 1) * DK], v_ref[:, h * DV:(h + 1) * DV],
            _col(b, h), r_t[h:h + 1, :], _col(inter, h), mask)
        num_ref[:, h * DV:(h + 1) * DV] = num
        den_all = jnp.where(lane == h, den, den_all)
        mt_all = jnp.where(lane == h, m_t, mt_all)
        wi_all = jnp.where(lane == h, w_int, wi_all)
    den_ref[...] = den_all
    mt_ref[...] = mt_all
    wi_ref[...] = wi_all
    wk_ref[...] = wk
    wkt_ref[...] = wk.T[0:8, :]
    dec_ref[...] = jnp.exp(b_last + m_prev - m_new)
    mnew_ref[...] = m_new


def _mlstm_sample_intra(q, k, v, gi, gf, mtok, steps):
    n = q.shape[0]
    L = CHUNK
    scal = jax.ShapeDtypeStruct((n, LANES), F32)
    return pl.pallas_call(
        functools.partial(_mlstm_sample_intra_body, steps),
        grid=(n // L,),
        in_specs=[_rows(L, D_QK), _rows(L, D_QK), _rows(L, D_M), _rows(L, LANES), _rows(L, LANES),
                  _rows(L, LANES)],
        out_specs=[_rows(L, D_M), _rows(L, LANES), _rows(L, LANES), _rows(L, LANES), _rows(L, LANES),
                   pl.BlockSpec((8, L), lambda i: (0, i)), _rows(L, LANES), _rows(L, LANES)],
        out_shape=[jax.ShapeDtypeStruct((n, D_M), F32), scal, scal, scal, scal,
                   jax.ShapeDtypeStruct((8, n), F32), scal, scal],
        compiler_params=_cparams(1),
        name="mlstm_sample_intra",
    )(q, k, v, gi, gf, mtok)


def _mlstm_sample_state_body(steps, q_ref, kwin_ref, vt_ref, so_ref, num_ref, den_ref, mt_ref, wi_ref,
                             wk_ref, wkt_ref, dec_ref, gmh_ref, c_ref, n_ref,
                             hmo_ref, cn_ref, nn_ref):
    rows = q_ref.shape[0]
    win = kwin_ref.shape[0]
    nseq = rows // steps
    first = (pl.program_id(0) % (win // rows)) * rows
    grp = lax.broadcasted_iota(jnp.int32, (rows, 1), 0) & -steps
    lane = lax.broadcasted_iota(jnp.int32, (1, win), 1)
    wi = wi_ref[...]
    for h in range(M_HEADS):
        qh = q_ref[:, h * DK:(h + 1) * DK]
        qf = qh.astype(F32)
        kwin = kwin_ref[:, h * DK:(h + 1) * DK]
        k_own = kwin_ref[pl.ds(pl.multiple_of(first, rows), rows), h * DK:(h + 1) * DK].astype(F32)
        vt = vt_ref[h * DV:(h + 1) * DV, :].astype(F32)
        wk_col = _col(wk_ref[...], h)
        wk_row = wkt_ref[h:h + 1, :]
        cq = jnp.zeros((rows, DV), F32)
        nq = jnp.zeros((rows, 1), F32)
        for s in range(nseq):
            c_old = c_ref[s, h]
            n_old = n_ref[s, h:h + 1, :]
            own = grp == s * steps
            cq = jnp.where(own, _dot_nt(qh, c_old.astype(BF16)), cq)
            nq = jnp.where(own, jnp.sum(qf * n_old, axis=1, keepdims=True), nq)
            dec = dec_ref[s * steps:s * steps + 1, h:h + 1]
            lo = first + s * steps
            wrow = jnp.where((lane >= lo) & (lane < lo + steps), wk_row, 0.0)
            cn_ref[s, h] = dec * c_old + _dot((vt * wrow).astype(BF16), kwin)
            nn_ref[s, h:h + 1, :] = dec * n_old + jnp.sum(
                k_own * jnp.where(own, wk_col, 0.0), axis=0, keepdims=True)
        w_int = _col(wi, h)
        num = num_ref[:, h * DV:(h + 1) * DV] + w_int * cq
        den = _col(den_ref[...], h) + w_int * nq
        hmo_ref[:, h * DV:(h + 1) * DV] = _head_out(
            num, den, _col(mt_ref[...], h), gmh_ref[:, h * DV:(h + 1) * DV], so_ref[:, h * DV:(h + 1) * DV])


def _mlstm_sample_state(q, k, vt, so, num, den, mt, wi, wk, wkt, dec, gmh, c0, n0, steps):
    n = q.shape[0]
    batch = c0.shape[0]
    rows = SEQ_PER_STEP * steps
    per_win = CHUNK // rows

    def win_rows(width):
        return pl.BlockSpec((CHUNK, width), lambda i: (i // per_win, 0))

    return pl.pallas_call(
        functools.partial(_mlstm_sample_state_body, steps),
        grid=(batch // SEQ_PER_STEP,),
        in_specs=[_rows(rows, D_QK), win_rows(D_QK),
                  pl.BlockSpec((D_M, CHUNK), lambda i: (0, i // per_win)),
                  _rows(rows, D_M), _rows(rows, D_M), _rows(rows, LANES), _rows(rows, LANES),
                  _rows(rows, LANES), _rows(rows, LANES),
                  pl.BlockSpec((8, CHUNK), lambda i: (0, i // per_win)),
                  _rows(rows, LANES), _resident((1, D_M)),
                  pl.BlockSpec((SEQ_PER_STEP, M_HEADS, DV, DK), lambda i: (i, 0, 0, 0)),
                  pl.BlockSpec((SEQ_PER_STEP, M_HEADS, DK), lambda i: (i, 0, 0))],
        out_specs=[_rows(rows, D_M),
                   pl.BlockSpec((SEQ_PER_STEP, M_HEADS, DV, DK), lambda i: (i, 0, 0, 0)),
                   pl.BlockSpec((SEQ_PER_STEP, M_HEADS, DK), lambda i: (i, 0, 0))],
        out_shape=[jax.ShapeDtypeStruct((n, D_M), BF16),
                   jax.ShapeDtypeStruct(c0.shape, F32),
                   jax.ShapeDtypeStruct(n0.shape, F32)],
        compiler_params=_cparams(1),
        name="mlstm_sample_state",
    )(q, k, vt, so, num, den, mt, wi, wk, wkt, dec, gmh, c0, n0)


def _out_body(x_ref, c_ref, hmo_ref, gc_ref, gm_ref, wco_ref, wmo_ref, wo_ref, gpm_ref, gpf_ref,
              w1_ref, b1_ref, w2_ref, b2_ref, gpo_ref, y_ref):
    y_conv = _dot(c_ref[...].astype(BF16), wco_ref[...])
    y_m = _dot(hmo_ref[...], wmo_ref[...])
    mix = gc_ref[...].astype(F32) * y_conv + gm_ref[...].astype(F32) * y_m
    x1 = x_ref[...] + _rmsnorm(_dot(mix.astype(BF16), wo_ref[...]), gpm_ref[...])
    hf = _rmsnorm(x1, gpf_ref[...]).astype(BF16)
    a = jnp.maximum(_dot(hf, w1_ref[...]) + b1_ref[...], 0.0)
    ff = _dot((a * a).astype(BF16), w2_ref[...]) + b2_ref[...]
    y_ref[...] = x1 + _rmsnorm(ff, gpo_ref[...])


def _out_ffn(x, c, hmo, gc, gm, wco, wmo, wo, gpm, gpf, w1, b1, w2, b2, gpo, tm):
    n = x.shape[0]
    return pl.pallas_call(
        _out_body,
        grid=(n // tm,),
        in_specs=[_rows(tm, D_MODEL), _rows(tm, D_MODEL), _rows(tm, D_M), _rows(tm, D_MODEL),
                  _rows(tm, D_MODEL),
                  _resident((D_MODEL, D_MODEL)), _resident((D_M, D_MODEL)), _resident((D_MODEL, D_MODEL)),
                  _resident((1, D_MODEL)), _resident((1, D_MODEL)),
                  _resident((D_MODEL, D_FF)), _resident((1, D_FF)),
                  _resident((D_FF, D_MODEL)), _resident((1, D_MODEL)), _resident((1, D_MODEL))],
        out_specs=_rows(tm, D_MODEL),
        out_shape=jax.ShapeDtypeStruct((n, D_MODEL), F32),
        compiler_params=_cparams(1),
        name="out_ffn",
    )(x, c, hmo, gc, gm, wco, wmo, wo, gpm, gpf, w1, b1, w2, b2, gpo)


def _row(v):
    return v.reshape(1, -1)


def _prep_layer(d, w_in, b_in, w_dw, w_conv_out, w_m_out, w_o, w_ff1, w_ff2):
    g0 = 2 * D_MODEL + 2 * D_QK + 2 * D_M
    wi, bi = w_in[d], b_in[d]
    w_main = jnp.concatenate([wi[:, :g0], wi[:, g0 + 2 * M_HEADS:]], axis=1).astype(BF16)
    b_main = _row(jnp.concatenate([bi[:g0], bi[g0 + 2 * M_HEADS:]]))
    w_g = jnp.zeros((D_MODEL, GATE_W), F32)
    w_g = w_g.at[:, :M_HEADS].set(wi[:, g0:g0 + M_HEADS])
    w_g = w_g.at[:, LANES:LANES + M_HEADS].set(wi[:, g0 + M_HEADS:g0 + 2 * M_HEADS]).astype(BF16)
    b_g = jnp.zeros((GATE_W,), F32)
    b_g = b_g.at[:M_HEADS].set(bi[g0:g0 + M_HEADS])
    b_g = _row(b_g.at[LANES:LANES + M_HEADS].set(bi[g0 + M_HEADS:g0 + 2 * M_HEADS]))
    wdw = jnp.concatenate([w_dw[d], jnp.zeros((HALO - CONV_WIDTH, D_MODEL), F32)], axis=0)
    return dict(w_main=w_main, b_main=b_main, w_g=w_g, b_g=b_g, wdw=wdw,
                wco=w_conv_out[d].astype(BF16), wmo=w_m_out[d].astype(BF16), wo=w_o[d].astype(BF16),
                w1=w_ff1[d].astype(BF16), w2=w_ff2[d].astype(BF16))


def kernel(x_prompt, x_sample, state_conv, state_C, state_n, state_m, g_pre_mix, w_in, b_in, w_dw, b_dw,
           g_cln, b_cln, w_conv_out, g_mh, w_m_out, w_o, g_post_mix, g_pre_ff, w_ff1, b_ff1, w_ff2, b_ff2,
           g_post_ff):
    bp, tp, _ = x_prompt.shape
    bs, ts, _ = x_sample.shape
    depth = w_in.shape[0]
    assert tp % CHUNK == 0 and tp >= CONV_BUF and CHUNK % (SEQ_PER_STEP * ts) == 0
    assert (bs * ts) % CHUNK == 0 and bs % SEQ_PER_STEP == 0
    tm_p = min(256, bp * tp)
    tm_s = min(256, bs * ts)
    assert (bp * tp) % tm_p == 0 and (bs * ts) % tm_s == 0
    yp = x_prompt.reshape(bp * tp, D_MODEL)
    ys = x_sample.reshape(bs * ts, D_MODEL)
    outs = [[] for _ in range(8)]
    for d in range(depth):
        p = _prep_layer(d, w_in, b_in, w_dw, w_conv_out, w_m_out, w_o, w_ff1, w_ff2)
        conv_args = (p["wdw"], _row(b_dw[d]), _row(g_cln[d]), _row(b_cln[d]))
        gmh = _row(g_mh[d])

        def tail(x, c, hmo, gc, gm, tm):
            return _out_ffn(x, c, hmo, gc, gm, p["wco"], p["wmo"], p["wo"], _row(g_post_mix[d]),
                            _row(g_pre_ff[d]), p["w1"], _row(b_ff1[d]), p["w2"], _row(b_ff2[d]),
                            _row(g_post_ff[d]), tm)

        u, q, k, v, so, gc, gm, gi, gf = _inproj(yp, _row(g_pre_mix[d]), p["w_main"], p["b_main"],
                                                 p["w_g"], p["b_g"], tm_p)
        c = _conv_prompt(u, bp, tp, *conv_args, CHUNK)
        hmo, c_new, n_new, m_new = _mlstm_prompt(q, k, v, so, gi, gf, gmh, bp, tp)
        yp = tail(yp, c, hmo, gc, gm, tm_p)
        outs[0].append(u.reshape(bp, tp, D_MODEL)[:, tp - CONV_BUF:, :])
        outs[1].append(c_new)
        outs[2].append(n_new)
        outs[3].append(m_new[:, 0, :M_HEADS])

        u, q, k, v, so, gc, gm, gi, gf = _inproj(ys, _row(g_pre_mix[d]), p["w_main"], p["b_main"],
                                                 p["w_g"], p["b_g"], tm_s)
        c = _conv_sample(state_conv[d], u, *conv_args, 8)
        mtok = jnp.pad(jnp.repeat(state_m[d], ts, axis=0), ((0, 0), (0, LANES - M_HEADS)))
        num, den, mt, wi, wk, wkt, dec, mnew = _mlstm_sample_intra(q, k, v, gi, gf, mtok, ts)
        hmo, c_new, n_new = _mlstm_sample_state(q, k, v.T, so, num, den, mt, wi, wk, wkt, dec, gmh,
                                                state_C[d], state_n[d], ts)
        ys = tail(ys, c, hmo, gc, gm, tm_s)
        outs[4].append(jnp.concatenate([state_conv[d][:, ts:, :], u.reshape(bs, ts, D_MODEL)], axis=1))
        outs[5].append(c_new)
        outs[6].append(n_new)
        outs[7].append(mnew[ts - 1::ts, :M_HEADS])
    return (yp.reshape(bp, tp, D_MODEL), ys.reshape(bs, ts, D_MODEL)) + tuple(jnp.stack(o) for o in outs)
```

```python
import functools

import jax
import jax.numpy as jnp
from jax import lax
from jax.experimental import pallas as pl
from jax.experimental.pallas import tpu as pltpu

F32 = jnp.float32
BF16 = jnp.bfloat16

D_MODEL = 1024
M_HEADS = 4
DK = 256
DV = 512
D_QK = M_HEADS * DK
D_M = M_HEADS * DV
D_FF = 4 * D_MODEL
CONV_WIDTH = 31
CONV_BUF = CONV_WIDTH - 1
CHUNK = 128
EPS = 1e-6
NEG = -1e30
LANES = 128
SUBLANES = 8
Q_COL = 2 * D_MODEL
K_COL = Q_COL + D_QK
V_COL = K_COL + D_QK
O_COL = V_COL + D_M
HEAD_COLS = O_COL + D_M
TAIL_W = -(-(2 * M_HEADS + 2 * D_MODEL) // LANES) * LANES
AUG = DV + LANES
HALO = 32
CONV_TOKENS = 8
CONV_ROWS = 16
NORM_UNROLL = 4
SEQ_PER_STEP = 4

VMEM_LIMIT = 56 * 1024 * 1024


def _cparams(n_axes):
    return pltpu.CompilerParams(dimension_semantics=("arbitrary",) * n_axes,
                                vmem_limit_bytes=VMEM_LIMIT)


def _resident(shape):
    nd = len(shape)
    return pl.BlockSpec(shape, lambda *_: (0,) * nd, pipeline_mode=pl.Buffered(1))


def _rows(tm, width):
    return pl.BlockSpec((tm, width), lambda i: (i, 0))


def _rmsnorm(x, g):
    return x * lax.rsqrt(jnp.mean(x * x, axis=-1, keepdims=True) + EPS) * g


def _sigmoid(x):
    return 1.0 / (1.0 + jnp.exp(-x))


def _log_sigmoid(x):
    return jnp.minimum(x, 0.0) - jnp.log(1.0 + jnp.exp(-jnp.abs(x)))


def _dot(a, b):
    return jnp.dot(a, b, preferred_element_type=F32)


def _dot_nt(a, b):
    return lax.dot_general(a, b, (((1,), (1,)), ((), ())), preferred_element_type=F32)


def _inproj_body(k_transposed, x_ref, g_ref, w_ref, b_ref, wk_ref, bk_ref,
                 u_ref, q_ref, k_ref, v_ref, so_ref, gc_ref, gm_ref, gi_ref, gf_ref):
    tm = x_ref.shape[0]
    hb = _rmsnorm(x_ref[...], g_ref[...]).astype(BF16)

    def seg(lo, hi):
        return _dot(hb, w_ref[:, lo:hi]) + b_ref[:, lo:hi]

    u = seg(0, D_MODEL) * _sigmoid(seg(D_MODEL, Q_COL))
    for c in range(SUBLANES):
        u_ref[pl.ds(c, tm, stride=SUBLANES), :] = u[:, c * LANES:(c + 1) * LANES]
    q_ref[...] = (seg(Q_COL, K_COL) * (DK ** -0.5)).astype(BF16)
    if k_transposed:
        k_ref[...] = (_dot_nt(wk_ref[...], hb) + bk_ref[:, 0:1]).astype(BF16)
    else:
        k_ref[...] = seg(K_COL, V_COL).astype(BF16)
    v_ref[...] = seg(V_COL, O_COL).astype(BF16)
    so_ref[...] = _sigmoid(seg(O_COL, HEAD_COLS)).astype(BF16)
    t = seg(HEAD_COLS, HEAD_COLS + TAIL_W)
    gi_ref[...] = t[:, :LANES]
    gf_ref[...] = _log_sigmoid(pltpu.roll(t[:, :LANES], LANES - M_HEADS, 1))
    t = pltpu.roll(t, TAIL_W - 2 * M_HEADS, 1)
    gc_ref[...] = _sigmoid(t[:, :D_MODEL]).astype(BF16)
    gm_ref[...] = _sigmoid(t[:, D_MODEL:2 * D_MODEL]).astype(BF16)


def _inproj(x, g, w, b, wk_t, bk_t, tm, k_transposed):
    n = x.shape[0]
    wcols = w.shape[1]
    specs = [(D_QK, BF16), None, (D_M, BF16), (D_M, BF16), (D_MODEL, BF16), (D_MODEL, BF16),
             (LANES, F32), (LANES, F32)]
    out_specs = [_rows(tm * SUBLANES, LANES)]
    out_shape = [jax.ShapeDtypeStruct((n * SUBLANES, LANES), F32)]
    for sp in specs:
        if sp is None and k_transposed:
            out_specs.append(pl.BlockSpec((D_QK, tm), lambda i: (0, i)))
            out_shape.append(jax.ShapeDtypeStruct((D_QK, n), BF16))
        else:
            wd, dt = sp or (D_QK, BF16)
            out_specs.append(_rows(tm, wd))
            out_shape.append(jax.ShapeDtypeStruct((n, wd), dt))
    return pl.pallas_call(
        functools.partial(_inproj_body, k_transposed),
        grid=(n // tm,),
        in_specs=[_rows(tm, D_MODEL), _resident((1, D_MODEL)), _resident((D_MODEL, wcols)),
                  _resident((1, wcols)), _resident(wk_t.shape), _resident(bk_t.shape)],
        out_specs=out_specs,
        out_shape=out_shape,
        compiler_params=_cparams(1),
        name="inproj",
    )(x, g, w, b, wk_t, bk_t)


def _ln_silu(c, g, b):
    mu = jnp.mean(c, axis=-1, keepdims=True)
    xc = c - mu
    y = xc * lax.rsqrt(jnp.mean(xc * xc, axis=-1, keepdims=True) + EPS) * g + b
    return y * _sigmoid(y)


def _conv_taps(win, w_ref, b_ref, n_out):
    accs = [b_ref[...]] * n_out
    for j in range(CONV_WIDTH):
        wj = w_ref[j * SUBLANES:(j + 1) * SUBLANES, :]
        accs = [accs[t] + win[t + j] * wj for t in range(n_out)]
    return accs


def _token_rows(buf, first_token, n):
    return jnp.concatenate(
        [buf[pl.ds(first_token * SUBLANES + c, n, stride=SUBLANES), :] for c in range(SUBLANES)], axis=1)


def _conv_prompt_body(u_ref, w_ref, b_ref, gln_ref, bln_ref, c_ref, ubuf, cbuf):
    tc = c_ref.shape[0]
    halo = HALO * SUBLANES

    @pl.when(pl.program_id(1) == 0)
    def _():
        ubuf[0:halo, :] = jnp.zeros((halo, LANES), F32)

    ubuf[halo:halo + tc * SUBLANES, :] = u_ref[...]

    def conv_block(g, carry):
        base = pl.multiple_of(g * (CONV_TOKENS * SUBLANES), CONV_TOKENS * SUBLANES)
        win = [ubuf[pl.ds(base + (HALO - CONV_BUF + k) * SUBLANES, SUBLANES), :]
               for k in range(CONV_TOKENS + CONV_BUF)]
        for t, acc in enumerate(_conv_taps(win, w_ref, b_ref, CONV_TOKENS)):
            cbuf[pl.ds(base + t * SUBLANES, SUBLANES), :] = acc
        return carry

    lax.fori_loop(0, tc // CONV_TOKENS, conv_block, 0)

    def norm_block(r, carry):
        r0 = pl.multiple_of(r * CONV_ROWS, CONV_ROWS)
        c_ref[pl.ds(r0, CONV_ROWS), :] = _ln_silu(
            _token_rows(cbuf, r0, CONV_ROWS), gln_ref[...], bln_ref[...]).astype(c_ref.dtype)
        return carry

    lax.fori_loop(0, tc // CONV_ROWS, norm_block, 0, unroll=NORM_UNROLL)
    ubuf[0:halo, :] = ubuf[tc * SUBLANES:tc * SUBLANES + halo, :]


def _conv_prompt(u, batch, seq, wdw, bdw, gln, bln, tc):
    nt = seq // tc
    return pl.pallas_call(
        _conv_prompt_body,
        grid=(batch, nt),
        in_specs=[pl.BlockSpec((tc * SUBLANES, LANES), lambda b, i: (b * nt + i, 0)),
                  _resident(wdw.shape), _resident(bdw.shape),
                  _resident((1, D_MODEL)), _resident((1, D_MODEL))],
        out_specs=pl.BlockSpec((tc, D_MODEL), lambda b, i: (b * nt + i, 0)),
        out_shape=jax.ShapeDtypeStruct((batch * seq, D_MODEL), BF16),
        scratch_shapes=[pltpu.VMEM(((HALO + tc) * SUBLANES, LANES), F32),
                        pltpu.VMEM((tc * SUBLANES, LANES), F32)],
        compiler_params=_cparams(2),
        name="conv_prompt",
    )(u, wdw, bdw, gln, bln)


def _conv_sample_body(steps, st_ref, u_ref, w_ref, b_ref, gln_ref, bln_ref, c_ref, cbuf):
    nseq = c_ref.shape[0] // steps
    hist = CONV_BUF * SUBLANES
    for s in range(nseq):
        win = [st_ref[pl.ds(s * hist + k * SUBLANES, SUBLANES), :] for k in range(CONV_BUF)]
        win += [u_ref[pl.ds((s * steps + k) * SUBLANES, SUBLANES), :] for k in range(steps)]
        for t, acc in enumerate(_conv_taps(win, w_ref, b_ref, steps)):
            cbuf[pl.ds((s * steps + t) * SUBLANES, SUBLANES), :] = acc
    c_ref[...] = _ln_silu(_token_rows(cbuf, 0, nseq * steps), gln_ref[...], bln_ref[...])


def _conv_sample(state, u, wdw, bdw, gln, bln, batch, nseq):
    steps = u.shape[0] // (batch * SUBLANES)
    return pl.pallas_call(
        functools.partial(_conv_sample_body, steps),
        grid=(batch // nseq,),
        in_specs=[_rows(nseq * CONV_BUF * SUBLANES, LANES), _rows(nseq * steps * SUBLANES, LANES),
                  _resident(wdw.shape), _resident(bdw.shape),
                  _resident((1, D_MODEL)), _resident((1, D_MODEL))],
        out_specs=_rows(nseq * steps, D_MODEL),
        out_shape=jax.ShapeDtypeStruct((batch * steps, D_MODEL), F32),
        scratch_shapes=[pltpu.VMEM((nseq * steps * SUBLANES, LANES), F32)],
        compiler_params=_cparams(1),
        name="conv_sample",
    )(state, u, wdw, bdw, gln, bln)


def _col(x, h):
    return x[:, h:h + 1]


def _with_ones(vh):
    lane = lax.broadcasted_iota(jnp.int32, (vh.shape[0], LANES), 1)
    return jnp.concatenate([vh, (lane == 0).astype(vh.dtype)], axis=1)


def _intra_head(s, vaug, b_col, r_row, inter_col, mask):
    dm = jnp.where(mask, b_col + r_row, NEG)
    m_t = jnp.maximum(inter_col, jnp.max(dm, axis=1, keepdims=True))
    sw = s * jnp.exp(dm - m_t)
    return _dot(sw.astype(BF16), vaug), m_t, jnp.exp(inter_col - m_t)


def _head_out(num_aug, m_t, gmh, so):
    num = num_aug[:, :DV]
    den = num_aug[:, DV:DV + 1]
    inv = 1.0 / jnp.maximum(jnp.abs(den), jnp.exp(-m_t))
    scale = inv * lax.rsqrt(inv * inv * jnp.mean(num * num, axis=-1, keepdims=True) + EPS)
    return (num * scale * (gmh * so.astype(F32))).astype(BF16)


def _mlstm_prompt_body(q_ref, kt_ref, v_ref, so_ref, gi_ref, gf_ref, gmh_ref,
                       hmo_ref, c_ref, n_ref, m_ref, ct_ref):
    L = q_ref.shape[0]
    chunk = pl.program_id(1)

    @pl.when(chunk == 0)
    def _():
        ct_ref[...] = jnp.zeros(ct_ref.shape, F32)
        m_ref[...] = jnp.zeros(m_ref.shape, F32)

    row = lax.broadcasted_iota(jnp.int32, (L, LANES), 0)
    gi = gi_ref[...]
    b = gf_ref[...]
    sh = 1
    while sh < L:
        b = b + jnp.where(row >= sh, pltpu.roll(b, sh, 0), 0.0)
        sh *= 2
    r = gi - b
    r_t = r.T
    m_prev = m_ref[0]
    b_last = b[L - 1:L, :]
    m_new = jnp.maximum(b_last + m_prev, jnp.max(b_last + r, axis=0, keepdims=True))
    dec = jnp.exp(b_last + m_prev - m_new)
    inter = b + m_prev
    tri = lax.broadcasted_iota(jnp.int32, (L, L), 1) <= lax.broadcasted_iota(jnp.int32, (L, L), 0)

    for h in range(M_HEADS):
        qh = q_ref[:, h * DK:(h + 1) * DK]
        kt = kt_ref[h * DK:(h + 1) * DK, :]
        vaug = _with_ones(v_ref[:, h * DV:(h + 1) * DV])
        num_aug, m_t, w_int = _intra_head(_dot(qh, kt), vaug, _col(b, h), r_t[h:h + 1, :], _col(inter, h), tri)
        ct_old = ct_ref[h]
        num_aug = num_aug + w_int * _dot(qh, ct_old.astype(BF16))
        hmo_ref[:, h * DV:(h + 1) * DV] = _head_out(
            num_aug, m_t, gmh_ref[:, h * DV:(h + 1) * DV], so_ref[:, h * DV:(h + 1) * DV])
        wk_row = jnp.exp(_col(b_last, h) + r_t[h:h + 1, :] - _col(m_new, h))
        ct_ref[h] = _col(dec, h) * ct_old + _dot((kt.astype(F32) * wk_row).astype(BF16), vaug)
    m_ref[0] = m_new

    @pl.when(chunk == pl.num_programs(1) - 1)
    def _():
        for h in range(M_HEADS):
            ct = ct_ref[h]
            c_ref[0, h] = ct[:, :DV].T
            n_ref[0, h:h + 1, :] = ct[:, DV:].T[0:1, :]


def _mlstm_prompt(q, kt, v, so, gi, gf, gmh, batch, seq):
    nc = seq // CHUNK
    n = batch * seq

    def tok(width):
        return pl.BlockSpec((CHUNK, width), lambda b, c: (b * nc + c, 0))

    return pl.pallas_call(
        _mlstm_prompt_body,
        grid=(batch, nc),
        in_specs=[tok(D_QK), pl.BlockSpec((D_QK, CHUNK), lambda b, c: (0, b * nc + c)),
                  tok(D_M), tok(D_M), tok(LANES), tok(LANES), _resident((1, D_M))],
        out_specs=[tok(D_M),
                   pl.BlockSpec((1, M_HEADS, DV, DK), lambda b, c: (b, 0, 0, 0)),
                   pl.BlockSpec((1, M_HEADS, DK), lambda b, c: (b, 0, 0)),
                   pl.BlockSpec((1, 1, LANES), lambda b, c: (b, 0, 0))],
        out_shape=[jax.ShapeDtypeStruct((n, D_M), BF16),
                   jax.ShapeDtypeStruct((batch, M_HEADS, DV, DK), F32),
                   jax.ShapeDtypeStruct((batch, M_HEADS, DK), F32),
                   jax.ShapeDtypeStruct((batch, 1, LANES), F32)],
        scratch_shapes=[pltpu.VMEM((M_HEADS, DK, AUG), F32)],
        compiler_params=_cparams(2),
        name="mlstm_prompt",
    )(q, kt, v, so, gi, gf, gmh)


def _mlstm_sample_intra_body(steps, q_ref, k_ref, v_ref, gi_ref, gf_ref, mtok_ref,
                             num_ref, mt_ref, wi_ref, wk_ref, wkt_ref, dec_ref, mnew_ref):
    L = q_ref.shape[0]
    row = lax.broadcasted_iota(jnp.int32, (L, LANES), 0)
    pos = row & (steps - 1)

    def down(x, k):
        return pltpu.roll(x, k, 0)

    def up(x, k):
        return pltpu.roll(x, L - k, 0)

    gi = gi_ref[...]
    b = gf_ref[...]
    sh = 1
    while sh < steps:
        b = b + jnp.where(pos >= sh, down(b, sh), 0.0)
        sh *= 2
    r = gi - b
    r_t = r.T
    m_prev = mtok_ref[...]
    b_last = b
    for k in range(1, steps):
        b_last = jnp.where(pos == steps - 1 - k, up(b, k), b_last)
    d_last = b_last + r
    d_max = d_last
    sh = 1
    while sh < steps:
        d_max = jnp.maximum(d_max, jnp.where((pos & sh) == 0, up(d_max, sh), down(d_max, sh)))
        sh *= 2
    m_new = jnp.maximum(b_last + m_prev, d_max)
    wk = jnp.exp(d_last - m_new)
    inter = b + m_prev
    ti = lax.broadcasted_iota(jnp.int32, (L, L), 0)
    si = lax.broadcasted_iota(jnp.int32, (L, L), 1)
    mask = (si <= ti) & ((si & -steps) == (ti & -steps))
    lane = lax.broadcasted_iota(jnp.int32, (L, LANES), 1)

    mt_all = jnp.zeros((L, LANES), F32)
    wi_all = jnp.zeros((L, LANES), F32)
    for h in range(M_HEADS):
        s = _dot_nt(q_ref[:, h * DK:(h + 1) * DK], k_ref[:, h * DK:(h + 1) * DK])
        num_aug, m_t, w_int = _intra_head(s, _with_ones(v_ref[:, h * DV:(h + 1) * DV]),
                                          _col(b, h), r_t[h:h + 1, :], _col(inter, h), mask)
        num_ref[:, h * AUG:(h + 1) * AUG] = num_aug
        mt_all = jnp.where(lane == h, m_t, mt_all)
        wi_all = jnp.where(lane == h, w_int, wi_all)
    mt_ref[...] = mt_all
    wi_ref[...] = wi_all
    wk_ref[...] = wk
    wkt_ref[...] = wk.T[0:8, :]
    dec_ref[...] = jnp.exp(b_last + m_prev - m_new)
    mnew_ref[...] = m_new


def _mlstm_sample_intra(q, k, v, gi, gf, mtok, steps):
    n = q.shape[0]
    L = CHUNK
    scal = jax.ShapeDtypeStruct((n, LANES), F32)
    return pl.pallas_call(
        functools.partial(_mlstm_sample_intra_body, steps),
        grid=(n // L,),
        in_specs=[_rows(L, D_QK), _rows(L, D_QK), _rows(L, D_M), _rows(L, LANES), _rows(L, LANES),
                  _rows(L, LANES)],
        out_specs=[_rows(L, M_HEADS * AUG), _rows(L, LANES), _rows(L, LANES), _rows(L, LANES),
                   pl.BlockSpec((8, L), lambda i: (0, i)), _rows(L, LANES), _rows(L, LANES)],
        out_shape=[jax.ShapeDtypeStruct((n, M_HEADS * AUG), F32), scal, scal, scal,
                   jax.ShapeDtypeStruct((8, n), F32), scal, scal],
        compiler_params=_cparams(1),
        name="mlstm_sample_intra",
    )(q, k, v, gi, gf, mtok)


def _mlstm_sample_state_body(steps, q_ref, kwin_ref, vt_ref, so_ref, num_ref, mt_ref, wi_ref,
                             wk_ref, wkt_ref, dec_ref, gmh_ref, c_ref, n_ref,
                             hmo_ref, cn_ref, nn_ref):
    rows = q_ref.shape[0]
    win = kwin_ref.shape[0]
    nseq = rows // steps
    first = (pl.program_id(0) % (win // rows)) * rows
    grp = lax.broadcasted_iota(jnp.int32, (rows, 1), 0) & -steps
    lane = lax.broadcasted_iota(jnp.int32, (1, win), 1)
    col = lax.broadcasted_iota(jnp.int32, (rows, LANES), 1)
    wi = wi_ref[...]
    for h in range(M_HEADS):
        qh = q_ref[:, h * DK:(h + 1) * DK]
        qf = qh.astype(F32)
        kwin = kwin_ref[:, h * DK:(h + 1) * DK]
        k_own = kwin_ref[pl.ds(pl.multiple_of(first, rows), rows), h * DK:(h + 1) * DK].astype(F32)
        vt = vt_ref[h * DV:(h + 1) * DV, :].astype(F32)
        wk_col = _col(wk_ref[...], h)
        wk_row = wkt_ref[h:h + 1, :]
        cq = jnp.zeros((rows, DV), F32)
        nq = jnp.zeros((rows, 1), F32)
        for s in range(nseq):
            c_old = c_ref[s, h]
            n_old = n_ref[s, h:h + 1, :]
            own = grp == s * steps
            cq = jnp.where(own, _dot_nt(qh, c_old.astype(BF16)), cq)
            nq = jnp.where(own, jnp.sum(qf * n_old, axis=1, keepdims=True), nq)
            dec = dec_ref[s * steps:s * steps + 1, h:h + 1]
            lo = first + s * steps
            wrow = jnp.where((lane >= lo) & (lane < lo + steps), wk_row, 0.0)
            cn_ref[s, h] = dec * c_old + _dot((vt * wrow).astype(BF16), kwin)
            nn_ref[s, h:h + 1, :] = dec * n_old + jnp.sum(
                k_own * jnp.where(own, wk_col, 0.0), axis=0, keepdims=True)
        inter_aug = jnp.concatenate([cq, jnp.where(col == 0, nq, 0.0)], axis=1)
        num_aug = num_ref[:, h * AUG:(h + 1) * AUG] + _col(wi, h) * inter_aug
        hmo_ref[:, h * DV:(h + 1) * DV] = _head_out(
            num_aug, _col(mt_ref[...], h), gmh_ref[:, h * DV:(h + 1) * DV], so_ref[:, h * DV:(h + 1) * DV])


def _mlstm_sample_state(q, k, vt, so, num, mt, wi, wk, wkt, dec, gmh, c0, n0, steps):
    n = q.shape[0]
    batch = c0.shape[0]
    rows = SEQ_PER_STEP * steps
    per_win = CHUNK // rows

    def win_rows(width):
        return pl.BlockSpec((CHUNK, width), lambda i: (i // per_win, 0))

    return pl.pallas_call(
        functools.partial(_mlstm_sample_state_body, steps),
        grid=(batch // SEQ_PER_STEP,),
        in_specs=[_rows(rows, D_QK), win_rows(D_QK),
                  pl.BlockSpec((D_M, CHUNK), lambda i: (0, i // per_win)),
                  _rows(rows, D_M), _rows(rows, M_HEADS * AUG), _rows(rows, LANES),
                  _rows(rows, LANES), _rows(rows, LANES),
                  pl.BlockSpec((8, CHUNK), lambda i: (0, i // per_win)),
                  _rows(rows, LANES), _resident((1, D_M)),
                  pl.BlockSpec((SEQ_PER_STEP, M_HEADS, DV, DK), lambda i: (i, 0, 0, 0)),
                  pl.BlockSpec((SEQ_PER_STEP, M_HEADS, DK), lambda i: (i, 0, 0))],
        out_specs=[_rows(rows, D_M),
                   pl.BlockSpec((SEQ_PER_STEP, M_HEADS, DV, DK), lambda i: (i, 0, 0, 0)),
                   pl.BlockSpec((SEQ_PER_STEP, M_HEADS, DK), lambda i: (i, 0, 0))],
        out_shape=[jax.ShapeDtypeStruct((n, D_M), BF16),
                   jax.ShapeDtypeStruct(c0.shape, F32),
                   jax.ShapeDtypeStruct(n0.shape, F32)],
        compiler_params=_cparams(1),
        name="mlstm_sample_state",
    )(q, k, vt, so, num, mt, wi, wk, wkt, dec, gmh, c0, n0)


def _out_body(x_ref, c_ref, hmo_ref, gc_ref, gm_ref, wco_ref, wmo_ref, wo_ref, gpm_ref, gpf_ref,
              w1_ref, b1_ref, w2_ref, b2_ref, gpo_ref, y_ref):
    y_conv = _dot(c_ref[...].astype(BF16), wco_ref[...])
    y_m = _dot(hmo_ref[...], wmo_ref[...])
    mix = gc_ref[...].astype(F32) * y_conv + gm_ref[...].astype(F32) * y_m
    x1 = x_ref[...] + _rmsnorm(_dot(mix.astype(BF16), wo_ref[...]), gpm_ref[...])
    hf = _rmsnorm(x1, gpf_ref[...]).astype(BF16)
    a = jnp.maximum(_dot(hf, w1_ref[...]) + b1_ref[...], 0.0)
    ff = _dot((a * a).astype(BF16), w2_ref[...]) + b2_ref[...]
    y_ref[...] = x1 + _rmsnorm(ff, gpo_ref[...])


def _out_ffn(x, c, hmo, gc, gm, wco, wmo, wo, gpm, gpf, w1, b1, w2, b2, gpo, tm):
    n = x.shape[0]
    return pl.pallas_call(
        _out_body,
        grid=(n // tm,),
        in_specs=[_rows(tm, D_MODEL), _rows(tm, D_MODEL), _rows(tm, D_M), _rows(tm, D_MODEL),
                  _rows(tm, D_MODEL),
                  _resident((D_MODEL, D_MODEL)), _resident((D_M, D_MODEL)), _resident((D_MODEL, D_MODEL)),
                  _resident((1, D_MODEL)), _resident((1, D_MODEL)),
                  _resident((D_MODEL, D_FF)), _resident((1, D_FF)),
                  _resident((D_FF, D_MODEL)), _resident((1, D_MODEL)), _resident((1, D_MODEL))],
        out_specs=_rows(tm, D_MODEL),
        out_shape=jax.ShapeDtypeStruct((n, D_MODEL), F32),
        compiler_params=_cparams(1),
        name="out_ffn",
    )(x, c, hmo, gc, gm, wco, wmo, wo, gpm, gpf, w1, b1, w2, b2, gpo)


def _row(v):
    return v.reshape(1, -1)


def _prep_layer(d, w_in, b_in, w_dw, w_conv_out, w_m_out, w_o, w_ff1, w_ff2):
    pad = HEAD_COLS + TAIL_W - w_in.shape[2]
    w_main = jnp.pad(w_in[d].astype(BF16), ((0, 0), (0, pad)))
    b_main = _row(jnp.pad(b_in[d], (0, pad)))
    wk_t = w_in[d, :, K_COL:V_COL].T.astype(BF16)
    bk_t = jnp.broadcast_to(b_in[d, K_COL:V_COL].reshape(D_QK, 1), (D_QK, LANES))
    wdw = w_dw[d].reshape(CONV_WIDTH * SUBLANES, LANES)
    return dict(w_main=w_main, b_main=b_main, wk_t=wk_t, bk_t=bk_t, wdw=wdw,
                wco=w_conv_out[d].astype(BF16), wmo=w_m_out[d].astype(BF16), wo=w_o[d].astype(BF16),
                w1=w_ff1[d].astype(BF16), w2=w_ff2[d].astype(BF16))


def kernel(x_prompt, x_sample, state_conv, state_C, state_n, state_m, g_pre_mix, w_in, b_in, w_dw, b_dw,
           g_cln, b_cln, w_conv_out, g_mh, w_m_out, w_o, g_post_mix, g_pre_ff, w_ff1, b_ff1, w_ff2, b_ff2,
           g_post_ff):
    bp, tp, _ = x_prompt.shape
    bs, ts, _ = x_sample.shape
    depth = w_in.shape[0]
    assert tp % CHUNK == 0 and tp >= CONV_BUF and CHUNK % (SEQ_PER_STEP * ts) == 0
    assert (bs * ts) % CHUNK == 0 and bs % SEQ_PER_STEP == 0 and ts & (ts - 1) == 0
    tm_p = min(256, bp * tp)
    tm_s = min(256, bs * ts)
    assert (bp * tp) % tm_p == 0 and (bs * ts) % tm_s == 0
    yp = x_prompt.reshape(bp * tp, D_MODEL)
    ys = x_sample.reshape(bs * ts, D_MODEL)
    outs = [[] for _ in range(8)]
    for d in range(depth):
        p = _prep_layer(d, w_in, b_in, w_dw, w_conv_out, w_m_out, w_o, w_ff1, w_ff2)
        conv_args = (p["wdw"], b_dw[d].reshape(SUBLANES, LANES), _row(g_cln[d]), _row(b_cln[d]))
        gmh = _row(g_mh[d])
        inproj_args = (_row(g_pre_mix[d]), p["w_main"], p["b_main"], p["wk_t"], p["bk_t"])

        def tail(x, c, hmo, gc, gm, tm):
            return _out_ffn(x, c, hmo, gc, gm, p["wco"], p["wmo"], p["wo"], _row(g_post_mix[d]),
                            _row(g_pre_ff[d]), p["w1"], _row(b_ff1[d]), p["w2"], _row(b_ff2[d]),
                            _row(g_post_ff[d]), tm)

        u, q, kt, v, so, gc, gm, gi, gf = _inproj(yp, *inproj_args, tm_p, True)
        c = _conv_prompt(u, bp, tp, *conv_args, 2 * CHUNK)
        hmo, c_new, n_new, m_new = _mlstm_prompt(q, kt, v, so, gi, gf, gmh, bp, tp)
        yp = tail(yp, c, hmo, gc, gm, tm_p)
        outs[0].append(u.reshape(bp, tp * SUBLANES, LANES)[:, (tp - CONV_BUF) * SUBLANES:, :]
                       .reshape(bp, CONV_BUF, D_MODEL))
        outs[1].append(c_new)
        outs[2].append(n_new)
        outs[3].append(m_new[:, 0, :M_HEADS])

        u, q, k, v, so, gc, gm, gi, gf = _inproj(ys, *inproj_args, tm_s, False)
        c = _conv_sample(state_conv[d].reshape(bs * CONV_BUF * SUBLANES, LANES), u, *conv_args, bs, 8)
        mtok = jnp.pad(jnp.repeat(state_m[d], ts, axis=0), ((0, 0), (0, LANES - M_HEADS)))
        num, mt, wi, wk, wkt, dec, mnew = _mlstm_sample_intra(q, k, v, gi, gf, mtok, ts)
        hmo, c_new, n_new = _mlstm_sample_state(q, k, v.T, so, num, mt, wi, wk, wkt, dec, gmh,
                                                state_C[d], state_n[d], ts)
        ys = tail(ys, c, hmo, gc, gm, tm_s)
        outs[4].append(jnp.concatenate(
            [state_conv[d][:, ts:, :], u.reshape(bs, ts, D_MODEL)], axis=1))
        outs[5].append(c_new)
        outs[6].append(n_new)
        outs[7].append(mnew[ts - 1::ts, :M_HEADS])
    return (yp.reshape(bp, tp, D_MODEL), ys.reshape(bs, ts, D_MODEL)) + tuple(jnp.stack(o) for o in outs)
```

```python
import functools

import jax
import jax.numpy as jnp
from jax import lax
from jax.experimental import pallas as pl
from jax.experimental.pallas import tpu as pltpu

F32 = jnp.float32
BF16 = jnp.bfloat16

D_MODEL = 1024
M_HEADS = 4
DK = 256
DV = 512
D_QK = M_HEADS * DK
D_M = M_HEADS * DV
D_FF = 4 * D_MODEL
CONV_WIDTH = 31
CONV_BUF = CONV_WIDTH - 1
CHUNK = 128
EPS = 1e-6
NEG = -1e30
LANES = 128
SUBLANES = 8
Q_COL = 2 * D_MODEL
K_COL = Q_COL + D_QK
V_COL = K_COL + D_QK
O_COL = V_COL + D_M
HEAD_COLS = O_COL + D_M
TAIL_W = -(-(2 * M_HEADS + 2 * D_MODEL) // LANES) * LANES
AUG = DV + LANES
HALO = 32
CONV_TOKENS = 8
CONV_ROWS = 16
NORM_UNROLL = 8
SEQ_PER_STEP = 4

VMEM_LIMIT = 56 * 1024 * 1024


def _cparams(n_axes):
    return pltpu.CompilerParams(dimension_semantics=("arbitrary",) * n_axes,
                                vmem_limit_bytes=VMEM_LIMIT)


def _resident(shape):
    nd = len(shape)
    return pl.BlockSpec(shape, lambda *_: (0,) * nd, pipeline_mode=pl.Buffered(1))


def _rows(tm, width):
    return pl.BlockSpec((tm, width), lambda i: (i, 0))


def _rmsnorm(x, g):
    return x * lax.rsqrt(jnp.mean(x * x, axis=-1, keepdims=True) + EPS) * g


def _sigmoid(x):
    return 1.0 / (1.0 + jnp.exp(-x))


def _log_sigmoid(x):
    return jnp.minimum(x, 0.0) - jnp.log(1.0 + jnp.exp(-jnp.abs(x)))


def _dot(a, b):
    return jnp.dot(a, b, preferred_element_type=F32)


def _dot_nt(a, b):
    return lax.dot_general(a, b, (((1,), (1,)), ((), ())), preferred_element_type=F32)


def _inproj_body(k_transposed, x_ref, g_ref, w_ref, b_ref, wk_ref, bk_ref,
                 u_ref, q_ref, k_ref, v_ref, so_ref, gc_ref, gm_ref, gi_ref, gf_ref):
    tm = x_ref.shape[0]
    hb = _rmsnorm(x_ref[...], g_ref[...]).astype(BF16)

    def seg(lo, hi):
        return _dot(hb, w_ref[:, lo:hi]) + b_ref[:, lo:hi]

    u = seg(0, D_MODEL) * _sigmoid(seg(D_MODEL, Q_COL))
    for c in range(SUBLANES):
        u_ref[pl.ds(c, tm, stride=SUBLANES), :] = u[:, c * LANES:(c + 1) * LANES]
    q_ref[...] = (seg(Q_COL, K_COL) * (DK ** -0.5)).astype(BF16)
    if k_transposed:
        k_ref[...] = (_dot_nt(wk_ref[...], hb) + bk_ref[:, 0:1]).astype(BF16)
    else:
        k_ref[...] = seg(K_COL, V_COL).astype(BF16)
    v_ref[...] = seg(V_COL, O_COL).astype(BF16)
    so_ref[...] = _sigmoid(seg(O_COL, HEAD_COLS)).astype(BF16)
    t = seg(HEAD_COLS, HEAD_COLS + TAIL_W)
    gi_ref[...] = t[:, :LANES]
    gf_ref[...] = _log_sigmoid(pltpu.roll(t[:, :LANES], LANES - M_HEADS, 1))
    t = pltpu.roll(t, TAIL_W - 2 * M_HEADS, 1)
    gc_ref[...] = _sigmoid(t[:, :D_MODEL]).astype(BF16)
    gm_ref[...] = _sigmoid(t[:, D_MODEL:2 * D_MODEL]).astype(BF16)


def _inproj(x, g, w, b, wk_t, bk_t, tm, k_transposed):
    n = x.shape[0]
    wcols = w.shape[1]
    specs = [(D_QK, BF16), None, (D_M, BF16), (D_M, BF16), (D_MODEL, BF16), (D_MODEL, BF16),
             (LANES, F32), (LANES, F32)]
    out_specs = [_rows(tm * SUBLANES, LANES)]
    out_shape = [jax.ShapeDtypeStruct((n * SUBLANES, LANES), F32)]
    for sp in specs:
        if sp is None and k_transposed:
            out_specs.append(pl.BlockSpec((D_QK, tm), lambda i: (0, i)))
            out_shape.append(jax.ShapeDtypeStruct((D_QK, n), BF16))
        else:
            wd, dt = sp or (D_QK, BF16)
            out_specs.append(_rows(tm, wd))
            out_shape.append(jax.ShapeDtypeStruct((n, wd), dt))
    return pl.pallas_call(
        functools.partial(_inproj_body, k_transposed),
        grid=(n // tm,),
        in_specs=[_rows(tm, D_MODEL), _resident((1, D_MODEL)), _resident((D_MODEL, wcols)),
                  _resident((1, wcols)), _resident(wk_t.shape), _resident(bk_t.shape)],
        out_specs=out_specs,
        out_shape=out_shape,
        compiler_params=_cparams(1),
        name="inproj",
    )(x, g, w, b, wk_t, bk_t)


def _ln_silu(c, g, b):
    mu = jnp.mean(c, axis=-1, keepdims=True)
    xc = c - mu
    y = xc * lax.rsqrt(jnp.mean(xc * xc, axis=-1, keepdims=True) + EPS) * g + b
    return y * _sigmoid(y)


def _conv_taps(win, w_ref, b_ref, n_out):
    even = [b_ref[...]] * n_out
    odd = [None] * n_out
    for j in range(CONV_WIDTH):
        wj = w_ref[j * SUBLANES:(j + 1) * SUBLANES, :]
        prods = [win[t + j] * wj for t in range(n_out)]
        if j % 2 == 0:
            even = [a + p for a, p in zip(even, prods)]
        else:
            odd = [p if a is None else a + p for a, p in zip(odd, prods)]
    return [a + b for a, b in zip(even, odd)]


def _token_rows(buf, first_token, n):
    return jnp.concatenate(
        [buf[pl.ds(first_token * SUBLANES + c, n, stride=SUBLANES), :] for c in range(SUBLANES)], axis=1)


def _conv_prompt_body(u_ref, w_ref, b_ref, gln_ref, bln_ref, c_ref, hist_ref, ubuf, cbuf):
    tc = c_ref.shape[0]
    halo = HALO * SUBLANES

    @pl.when(pl.program_id(1) == 0)
    def _():
        ubuf[0:halo, :] = jnp.zeros((halo, LANES), F32)

    ubuf[halo:halo + tc * SUBLANES, :] = u_ref[...]

    def conv_block(g, carry):
        base = pl.multiple_of(g * (CONV_TOKENS * SUBLANES), CONV_TOKENS * SUBLANES)
        win = [ubuf[pl.ds(base + (HALO - CONV_BUF + k) * SUBLANES, SUBLANES), :]
               for k in range(CONV_TOKENS + CONV_BUF)]
        for t, acc in enumerate(_conv_taps(win, w_ref, b_ref, CONV_TOKENS)):
            cbuf[pl.ds(base + t * SUBLANES, SUBLANES), :] = acc
        return carry

    lax.fori_loop(0, tc // CONV_TOKENS, conv_block, 0)

    def norm_block(r, carry):
        r0 = pl.multiple_of(r * CONV_ROWS, CONV_ROWS)
        c_ref[pl.ds(r0, CONV_ROWS), :] = _ln_silu(
            _token_rows(cbuf, r0, CONV_ROWS), gln_ref[...], bln_ref[...]).astype(c_ref.dtype)
        return carry

    lax.fori_loop(0, tc // CONV_ROWS, norm_block, 0, unroll=NORM_UNROLL)

    @pl.when(pl.program_id(1) == pl.num_programs(1) - 1)
    def _():
        hist_ref[0] = _token_rows(ubuf, HALO + tc - CONV_BUF, CONV_BUF)

    ubuf[0:halo, :] = ubuf[tc * SUBLANES:tc * SUBLANES + halo, :]


def _conv_prompt(u, batch, seq, wdw, bdw, gln, bln, tc):
    nt = seq // tc
    return pl.pallas_call(
        _conv_prompt_body,
        grid=(batch, nt),
        in_specs=[pl.BlockSpec((tc * SUBLANES, LANES), lambda b, i: (b * nt + i, 0)),
                  _resident(wdw.shape), _resident(bdw.shape),
                  _resident((1, D_MODEL)), _resident((1, D_MODEL))],
        out_specs=[pl.BlockSpec((tc, D_MODEL), lambda b, i: (b * nt + i, 0)),
                   pl.BlockSpec((1, CONV_BUF, D_MODEL), lambda b, i: (b, 0, 0))],
        out_shape=[jax.ShapeDtypeStruct((batch * seq, D_MODEL), BF16),
                   jax.ShapeDtypeStruct((batch, CONV_BUF, D_MODEL), F32)],
        scratch_shapes=[pltpu.VMEM(((HALO + tc) * SUBLANES, LANES), F32),
                        pltpu.VMEM((tc * SUBLANES, LANES), F32)],
        compiler_params=_cparams(2),
        name="conv_prompt",
    )(u, wdw, bdw, gln, bln)


def _conv_sample_body(steps, st_ref, u_ref, w_ref, b_ref, gln_ref, bln_ref, c_ref, hist_ref, fbuf, cbuf):
    nseq = st_ref.shape[0]
    full = CONV_BUF + steps
    for s in range(nseq):
        f0 = s * full
        for c in range(SUBLANES):
            fbuf[pl.ds(f0 * SUBLANES + c, CONV_BUF, stride=SUBLANES), :] = st_ref[s, :, c * LANES:(c + 1) * LANES]
        fbuf[(f0 + CONV_BUF) * SUBLANES:(f0 + full) * SUBLANES, :] = (
            u_ref[s * steps * SUBLANES:(s + 1) * steps * SUBLANES, :])
        win = [fbuf[(f0 + k) * SUBLANES:(f0 + k + 1) * SUBLANES, :] for k in range(full)]
        for t, acc in enumerate(_conv_taps(win, w_ref, b_ref, steps)):
            cbuf[(s * steps + t) * SUBLANES:(s * steps + t + 1) * SUBLANES, :] = acc
        hist_ref[s] = _token_rows(fbuf, f0 + steps, CONV_BUF)
    c_ref[...] = _ln_silu(_token_rows(cbuf, 0, nseq * steps), gln_ref[...], bln_ref[...])


def _conv_sample(state, u, wdw, bdw, gln, bln, nseq):
    batch = state.shape[0]
    steps = u.shape[0] // (batch * SUBLANES)
    return pl.pallas_call(
        functools.partial(_conv_sample_body, steps),
        grid=(batch // nseq,),
        in_specs=[pl.BlockSpec((nseq, CONV_BUF, D_MODEL), lambda i: (i, 0, 0)),
                  _rows(nseq * steps * SUBLANES, LANES),
                  _resident(wdw.shape), _resident(bdw.shape),
                  _resident((1, D_MODEL)), _resident((1, D_MODEL))],
        out_specs=[_rows(nseq * steps, D_MODEL),
                   pl.BlockSpec((nseq, CONV_BUF, D_MODEL), lambda i: (i, 0, 0))],
        out_shape=[jax.ShapeDtypeStruct((batch * steps, D_MODEL), F32),
                   jax.ShapeDtypeStruct(state.shape, F32)],
        scratch_shapes=[pltpu.VMEM((nseq * (CONV_BUF + steps) * SUBLANES, LANES), F32),
                        pltpu.VMEM((nseq * steps * SUBLANES, LANES), F32)],
        compiler_params=_cparams(1),
        name="conv_sample",
    )(state, u, wdw, bdw, gln, bln)


def _col(x, h):
    return x[:, h:h + 1]


def _with_ones(vh):
    lane = lax.broadcasted_iota(jnp.int32, (vh.shape[0], LANES), 1)
    return jnp.concatenate([vh, (lane == 0).astype(vh.dtype)], axis=1)


def _intra_head(s, vaug, b_col, r_row, inter_col, mask):
    dm = jnp.where(mask, b_col + r_row, NEG)
    m_t = jnp.maximum(inter_col, jnp.max(dm, axis=1, keepdims=True))
    sw = s * jnp.exp(dm - m_t)
    return _dot(sw.astype(BF16), vaug), m_t, jnp.exp(inter_col - m_t)


def _head_out(num_aug, m_t, gmh, so):
    num = num_aug[:, :DV]
    den = num_aug[:, DV:DV + 1]
    inv = 1.0 / jnp.maximum(jnp.abs(den), jnp.exp(-m_t))
    scale = inv * lax.rsqrt(inv * inv * jnp.mean(num * num, axis=-1, keepdims=True) + EPS)
    return (num * scale * (gmh * so.astype(F32))).astype(BF16)


def _mlstm_prompt_body(q_ref, kt_ref, v_ref, so_ref, gi_ref, gf_ref, gmh_ref,
                       hmo_ref, c_ref, n_ref, m_ref, ct_ref):
    L = q_ref.shape[0]
    chunk = pl.program_id(1)

    @pl.when(chunk == 0)
    def _():
        ct_ref[...] = jnp.zeros(ct_ref.shape, F32)
        m_ref[...] = jnp.zeros(m_ref.shape, F32)

    row = lax.broadcasted_iota(jnp.int32, (L, LANES), 0)
    gi = gi_ref[...]
    b = gf_ref[...]
    sh = 1
    while sh < L:
        b = b + jnp.where(row >= sh, pltpu.roll(b, sh, 0), 0.0)
        sh *= 2
    r = gi - b
    r_t = r.T
    m_prev = m_ref[0]
    b_last = b[L - 1:L, :]
    m_new = jnp.maximum(b_last + m_prev, jnp.max(b_last + r, axis=0, keepdims=True))
    dec = jnp.exp(b_last + m_prev - m_new)
    inter = b + m_prev
    tri = lax.broadcasted_iota(jnp.int32, (L, L), 1) <= lax.broadcasted_iota(jnp.int32, (L, L), 0)

    for h in range(M_HEADS):
        qh = q_ref[:, h * DK:(h + 1) * DK]
        kt = kt_ref[h * DK:(h + 1) * DK, :]
        vaug = _with_ones(v_ref[:, h * DV:(h + 1) * DV])
        num_aug, m_t, w_int = _intra_head(_dot(qh, kt), vaug, _col(b, h), r_t[h:h + 1, :], _col(inter, h), tri)
        ct_old = ct_ref[h]
        num_aug = num_aug + w_int * _dot(qh, ct_old.astype(BF16))
        hmo_ref[:, h * DV:(h + 1) * DV] = _head_out(
            num_aug, m_t, gmh_ref[:, h * DV:(h + 1) * DV], so_ref[:, h * DV:(h + 1) * DV])
        wk_row = jnp.exp(_col(b_last, h) + r_t[h:h + 1, :] - _col(m_new, h))
        ct_ref[h] = _col(dec, h) * ct_old + _dot((kt.astype(F32) * wk_row).astype(BF16), vaug)
    m_ref[0] = m_new

    @pl.when(chunk == pl.num_programs(1) - 1)
    def _():
        for h in range(M_HEADS):
            ct = ct_ref[h]
            c_ref[0, h] = ct[:, :DV].T
            n_ref[0, h:h + 1, :] = ct[:, DV:].T[0:1, :]


def _mlstm_prompt(q, kt, v, so, gi, gf, gmh, batch, seq):
    nc = seq // CHUNK
    n = batch * seq

    def tok(width):
        return pl.BlockSpec((CHUNK, width), lambda b, c: (b * nc + c, 0))

    return pl.pallas_call(
        _mlstm_prompt_body,
        grid=(batch, nc),
        in_specs=[tok(D_QK), pl.BlockSpec((D_QK, CHUNK), lambda b, c: (0, b * nc + c)),
                  tok(D_M), tok(D_M), tok(LANES), tok(LANES), _resident((1, D_M))],
        out_specs=[tok(D_M),
                   pl.BlockSpec((1, M_HEADS, DV, DK), lambda b, c: (b, 0, 0, 0)),
                   pl.BlockSpec((1, M_HEADS, DK), lambda b, c: (b, 0, 0)),
                   pl.BlockSpec((1, 1, LANES), lambda b, c: (b, 0, 0))],
        out_shape=[jax.ShapeDtypeStruct((n, D_M), BF16),
                   jax.ShapeDtypeStruct((batch, M_HEADS, DV, DK), F32),
                   jax.ShapeDtypeStruct((batch, M_HEADS, DK), F32),
                   jax.ShapeDtypeStruct((batch, 1, LANES), F32)],
        scratch_shapes=[pltpu.VMEM((M_HEADS, DK, AUG), F32)],
        compiler_params=_cparams(2),
        name="mlstm_prompt",
    )(q, kt, v, so, gi, gf, gmh)


def _mlstm_sample_intra_body(steps, q_ref, k_ref, v_ref, gi_ref, gf_ref, mtok_ref,
                             num_ref, mt_ref, wi_ref, wk_ref, wkt_ref, dec_ref, mnew_ref):
    L = q_ref.shape[0]
    row = lax.broadcasted_iota(jnp.int32, (L, LANES), 0)
    pos = row & (steps - 1)

    def down(x, k):
        return pltpu.roll(x, k, 0)

    def up(x, k):
        return pltpu.roll(x, L - k, 0)

    gi = gi_ref[...]
    b = gf_ref[...]
    sh = 1
    while sh < steps:
        b = b + jnp.where(pos >= sh, down(b, sh), 0.0)
        sh *= 2
    r = gi - b
    r_t = r.T
    m_prev = mtok_ref[...]
    b_last = b
    for k in range(1, steps):
        b_last = jnp.where(pos == steps - 1 - k, up(b, k), b_last)
    d_last = b_last + r
    d_max = d_last
    sh = 1
    while sh < steps:
        d_max = jnp.maximum(d_max, jnp.where((pos & sh) == 0, up(d_max, sh), down(d_max, sh)))
        sh *= 2
    m_new = jnp.maximum(b_last + m_prev, d_max)
    wk = jnp.exp(d_last - m_new)
    inter = b + m_prev
    ti = lax.broadcasted_iota(jnp.int32, (L, L), 0)
    si = lax.broadcasted_iota(jnp.int32, (L, L), 1)
    mask = (si <= ti) & ((si & -steps) == (ti & -steps))
    lane = lax.broadcasted_iota(jnp.int32, (L, LANES), 1)

    mt_all = jnp.zeros((L, LANES), F32)
    wi_all = jnp.zeros((L, LANES), F32)
    for h in range(M_HEADS):
        s = _dot_nt(q_ref[:, h * DK:(h + 1) * DK], k_ref[:, h * DK:(h + 1) * DK])
        num_aug, m_t, w_int = _intra_head(s, _with_ones(v_ref[:, h * DV:(h + 1) * DV]),
                                          _col(b, h), r_t[h:h + 1, :], _col(inter, h), mask)
        num_ref[:, h * AUG:(h + 1) * AUG] = num_aug
        mt_all = jnp.where(lane == h, m_t, mt_all)
        wi_all = jnp.where(lane == h, w_int, wi_all)
    mt_ref[...] = mt_all
    wi_ref[...] = wi_all
    wk_ref[...] = wk
    wkt_ref[...] = wk.T[0:8, :]
    dec_ref[...] = jnp.exp(b_last + m_prev - m_new)
    mnew_ref[...] = m_new


def _mlstm_sample_intra(q, k, v, gi, gf, mtok, steps):
    n = q.shape[0]
    L = CHUNK
    scal = jax.ShapeDtypeStruct((n, LANES), F32)
    return pl.pallas_call(
        functools.partial(_mlstm_sample_intra_body, steps),
        grid=(n // L,),
        in_specs=[_rows(L, D_QK), _rows(L, D_QK), _rows(L, D_M), _rows(L, LANES), _rows(L, LANES),
                  _rows(L, LANES)],
        out_specs=[_rows(L, M_HEADS * AUG), _rows(L, LANES), _rows(L, LANES), _rows(L, LANES),
                   pl.BlockSpec((8, L), lambda i: (0, i)), _rows(L, LANES), _rows(L, LANES)],
        out_shape=[jax.ShapeDtypeStruct((n, M_HEADS * AUG), F32), scal, scal, scal,
                   jax.ShapeDtypeStruct((8, n), F32), scal, scal],
        compiler_params=_cparams(1),
        name="mlstm_sample_intra",
    )(q, k, v, gi, gf, mtok)


def _mlstm_sample_state_body(steps, q_ref, kwin_ref, vt_ref, so_ref, num_ref, mt_ref, wi_ref,
                             wk_ref, wkt_ref, dec_ref, gmh_ref, c_ref, n_ref,
                             hmo_ref, cn_ref, nn_ref):
    rows = q_ref.shape[0]
    win = kwin_ref.shape[0]
    nseq = rows // steps
    first = (pl.program_id(0) % (win // rows)) * rows
    grp = lax.broadcasted_iota(jnp.int32, (rows, 1), 0) & -steps
    lane = lax.broadcasted_iota(jnp.int32, (1, win), 1)
    col = lax.broadcasted_iota(jnp.int32, (rows, LANES), 1)
    wi = wi_ref[...]
    for h in range(M_HEADS):
        qh = q_ref[:, h * DK:(h + 1) * DK]
        qf = qh.astype(F32)
        kwin = kwin_ref[:, h * DK:(h + 1) * DK]
        k_own = kwin_ref[pl.ds(pl.multiple_of(first, rows), rows), h * DK:(h + 1) * DK].astype(F32)
        vt = vt_ref[h * DV:(h + 1) * DV, :].astype(F32)
        wk_col = _col(wk_ref[...], h)
        wk_row = wkt_ref[h:h + 1, :]
        cq = jnp.zeros((rows, DV), F32)
        nq = jnp.zeros((rows, 1), F32)
        for s in range(nseq):
            c_old = c_ref[s, h]
            n_old = n_ref[s, h:h + 1, :]
            own = grp == s * steps
            cq = jnp.where(own, _dot_nt(qh, c_old.astype(BF16)), cq)
            nq = jnp.where(own, jnp.sum(qf * n_old, axis=1, keepdims=True), nq)
            dec = dec_ref[s * steps:s * steps + 1, h:h + 1]
            lo = first + s * steps
            wrow = jnp.where((lane >= lo) & (lane < lo + steps), wk_row, 0.0)
            cn_ref[s, h] = dec * c_old + _dot((vt * wrow).astype(BF16), kwin)
            nn_ref[s, h:h + 1, :] = dec * n_old + jnp.sum(
                k_own * jnp.where(own, wk_col, 0.0), axis=0, keepdims=True)
        inter_aug = jnp.concatenate([cq, jnp.where(col == 0, nq, 0.0)], axis=1)
        num_aug = num_ref[:, h * AUG:(h + 1) * AUG] + _col(wi, h) * inter_aug
        hmo_ref[:, h * DV:(h + 1) * DV] = _head_out(
            num_aug, _col(mt_ref[...], h), gmh_ref[:, h * DV:(h + 1) * DV], so_ref[:, h * DV:(h + 1) * DV])


def _mlstm_sample_state(q, k, vt, so, num, mt, wi, wk, wkt, dec, gmh, c0, n0, steps):
    n = q.shape[0]
    batch = c0.shape[0]
    rows = SEQ_PER_STEP * steps
    per_win = CHUNK // rows

    def win_rows(width):
        return pl.BlockSpec((CHUNK, width), lambda i: (i // per_win, 0))

    return pl.pallas_call(
        functools.partial(_mlstm_sample_state_body, steps),
        grid=(batch // SEQ_PER_STEP,),
        in_specs=[_rows(rows, D_QK), win_rows(D_QK),
                  pl.BlockSpec((D_M, CHUNK), lambda i: (0, i // per_win)),
                  _rows(rows, D_M), _rows(rows, M_HEADS * AUG), _rows(rows, LANES),
                  _rows(rows, LANES), _rows(rows, LANES),
                  pl.BlockSpec((8, CHUNK), lambda i: (0, i // per_win)),
                  _rows(rows, LANES), _resident((1, D_M)),
                  pl.BlockSpec((SEQ_PER_STEP, M_HEADS, DV, DK), lambda i: (i, 0, 0, 0)),
                  pl.BlockSpec((SEQ_PER_STEP, M_HEADS, DK), lambda i: (i, 0, 0))],
        out_specs=[_rows(rows, D_M),
                   pl.BlockSpec((SEQ_PER_STEP, M_HEADS, DV, DK), lambda i: (i, 0, 0, 0)),
                   pl.BlockSpec((SEQ_PER_STEP, M_HEADS, DK), lambda i: (i, 0, 0))],
        out_shape=[jax.ShapeDtypeStruct((n, D_M), BF16),
                   jax.ShapeDtypeStruct(c0.shape, F32),
                   jax.ShapeDtypeStruct(n0.shape, F32)],
        compiler_params=_cparams(1),
        name="mlstm_sample_state",
    )(q, k, vt, so, num, mt, wi, wk, wkt, dec, gmh, c0, n0)


def _merge_body(x_ref, c_ref, hmo_ref, gc_ref, gm_ref, wco_ref, wmo_ref, wo_ref, gpm_ref, y_ref):
    y_conv = _dot(c_ref[...].astype(BF16), wco_ref[...])
    y_m = _dot(hmo_ref[...], wmo_ref[...])
    mix = gc_ref[...].astype(F32) * y_conv + gm_ref[...].astype(F32) * y_m
    y_ref[...] = x_ref[...] + _rmsnorm(_dot(mix.astype(BF16), wo_ref[...]), gpm_ref[...])


def _merge(x, c, hmo, gc, gm, wco, wmo, wo, gpm, tm):
    n = x.shape[0]
    return pl.pallas_call(
        _merge_body,
        grid=(n // tm,),
        in_specs=[_rows(tm, D_MODEL), _rows(tm, D_MODEL), _rows(tm, D_M), _rows(tm, D_MODEL),
                  _rows(tm, D_MODEL),
                  _resident((D_MODEL, D_MODEL)), _resident((D_M, D_MODEL)), _resident((D_MODEL, D_MODEL)),
                  _resident((1, D_MODEL))],
        out_specs=_rows(tm, D_MODEL),
        out_shape=jax.ShapeDtypeStruct((n, D_MODEL), F32),
        compiler_params=_cparams(1),
        name="merge",
    )(x, c, hmo, gc, gm, wco, wmo, wo, gpm)


def _ffn_body(x_ref, gpf_ref, w1_ref, b1_ref, w2_ref, b2_ref, gpo_ref, y_ref):
    x1 = x_ref[...]
    hf = _rmsnorm(x1, gpf_ref[...]).astype(BF16)
    a = jnp.maximum(_dot(hf, w1_ref[...]) + b1_ref[...], 0.0)
    ff = _dot((a * a).astype(BF16), w2_ref[...]) + b2_ref[...]
    y_ref[...] = x1 + _rmsnorm(ff, gpo_ref[...])


def _ffn(x, gpf, w1, b1, w2, b2, gpo, tm):
    n = x.shape[0]
    return pl.pallas_call(
        _ffn_body,
        grid=(n // tm,),
        in_specs=[_rows(tm, D_MODEL), _resident((1, D_MODEL)),
                  _resident((D_MODEL, D_FF)), _resident((1, D_FF)),
                  _resident((D_FF, D_MODEL)), _resident((1, D_MODEL)), _resident((1, D_MODEL))],
        out_specs=_rows(tm, D_MODEL),
        out_shape=jax.ShapeDtypeStruct((n, D_MODEL), F32),
        compiler_params=_cparams(1),
        name="ffn",
    )(x, gpf, w1, b1, w2, b2, gpo)


def _row(v):
    return v.reshape(1, -1)


def _prep_layer(d, w_in, b_in, w_dw, w_conv_out, w_m_out, w_o, w_ff1, w_ff2):
    pad = HEAD_COLS + TAIL_W - w_in.shape[2]
    w_main = jnp.pad(w_in[d].astype(BF16), ((0, 0), (0, pad)))
    b_main = _row(jnp.pad(b_in[d], (0, pad)))
    wk_t = w_in[d, :, K_COL:V_COL].T.astype(BF16)
    bk_t = jnp.broadcast_to(b_in[d, K_COL:V_COL].reshape(D_QK, 1), (D_QK, LANES))
    wdw = w_dw[d].reshape(CONV_WIDTH * SUBLANES, LANES)
    return dict(w_main=w_main, b_main=b_main, wk_t=wk_t, bk_t=bk_t, wdw=wdw,
                wco=w_conv_out[d].astype(BF16), wmo=w_m_out[d].astype(BF16), wo=w_o[d].astype(BF16),
                w1=w_ff1[d].astype(BF16), w2=w_ff2[d].astype(BF16))


def kernel(x_prompt, x_sample, state_conv, state_C, state_n, state_m, g_pre_mix, w_in, b_in, w_dw, b_dw,
           g_cln, b_cln, w_conv_out, g_mh, w_m_out, w_o, g_post_mix, g_pre_ff, w_ff1, b_ff1, w_ff2, b_ff2,
           g_post_ff):
    bp, tp, _ = x_prompt.shape
    bs, ts, _ = x_sample.shape
    depth = w_in.shape[0]
    assert tp % CHUNK == 0 and tp >= CONV_BUF and CHUNK % (SEQ_PER_STEP * ts) == 0
    assert (bs * ts) % CHUNK == 0 and bs % SEQ_PER_STEP == 0 and ts & (ts - 1) == 0
    tm_p = min(256, bp * tp)
    tm_s = min(256, bs * ts)
    assert (bp * tp) % tm_p == 0 and (bs * ts) % tm_s == 0
    yp = x_prompt.reshape(bp * tp, D_MODEL)
    ys = x_sample.reshape(bs * ts, D_MODEL)
    outs = [[] for _ in range(8)]
    for d in range(depth):
        p = _prep_layer(d, w_in, b_in, w_dw, w_conv_out, w_m_out, w_o, w_ff1, w_ff2)
        conv_args = (p["wdw"], b_dw[d].reshape(SUBLANES, LANES), _row(g_cln[d]), _row(b_cln[d]))
        gmh = _row(g_mh[d])
        inproj_args = (_row(g_pre_mix[d]), p["w_main"], p["b_main"], p["wk_t"], p["bk_t"])

        def tail(x, c, hmo, gc, gm, tm):
            x1 = _merge(x, c, hmo, gc, gm, p["wco"], p["wmo"], p["wo"], _row(g_post_mix[d]), tm)
            return _ffn(x1, _row(g_pre_ff[d]), p["w1"], _row(b_ff1[d]), p["w2"], _row(b_ff2[d]),
                        _row(g_post_ff[d]), tm)

        u, q, kt, v, so, gc, gm, gi, gf = _inproj(yp, *inproj_args, tm_p, True)
        c, hist = _conv_prompt(u, bp, tp, *conv_args, 2 * CHUNK)
        hmo, c_new, n_new, m_new = _mlstm_prompt(q, kt, v, so, gi, gf, gmh, bp, tp)
        yp = tail(yp, c, hmo, gc, gm, 2 * tm_p)
        outs[0].append(hist)
        outs[1].append(c_new)
        outs[2].append(n_new)
        outs[3].append(m_new[:, 0, :M_HEADS])

        u, q, k, v, so, gc, gm, gi, gf = _inproj(ys, *inproj_args, tm_s, False)
        c, hist = _conv_sample(state_conv[d], u, *conv_args, 8)
        mtok = jnp.pad(jnp.repeat(state_m[d], ts, axis=0), ((0, 0), (0, LANES - M_HEADS)))
        num, mt, wi, wk, wkt, dec, mnew = _mlstm_sample_intra(q, k, v, gi, gf, mtok, ts)
        hmo, c_new, n_new = _mlstm_sample_state(q, k, v.T, so, num, mt, wi, wk, wkt, dec, gmh,
                                                state_C[d], state_n[d], ts)
        ys = tail(ys, c, hmo, gc, gm, tm_s)
        outs[4].append(hist)
        outs[5].append(c_new)
        outs[6].append(n_new)
        outs[7].append(mnew[ts - 1::ts, :M_HEADS])
    return (yp.reshape(bp, tp, D_MODEL), ys.reshape(bs, ts, D_MODEL)) + tuple(jnp.stack(o) for o in outs)
```

```python
import functools

import jax
import jax.numpy as jnp
from jax import lax
from jax.experimental import pallas as pl
from jax.experimental.pallas import tpu as pltpu

F32 = jnp.float32
BF16 = jnp.bfloat16

D_MODEL = 1024
M_HEADS = 4
DK = 256
DV = 512
D_QK = M_HEADS * DK
D_M = M_HEADS * DV
D_FF = 4 * D_MODEL
CONV_WIDTH = 31
CONV_BUF = CONV_WIDTH - 1
CHUNK = 128
EPS = 1e-6
NEG = -1e30
LANES = 128
SUBLANES = 8
Q_COL = 2 * D_MODEL
K_COL = Q_COL + D_QK
V_COL = K_COL + D_QK
O_COL = V_COL + D_M
HEAD_COLS = O_COL + D_M
TAIL_W = -(-(2 * M_HEADS + 2 * D_MODEL) // LANES) * LANES
AUG = DV + LANES
HALO = 32
CONV_TOKENS = 8
CONV_ROWS = 16
SEQ_PER_STEP = 4

VMEM_LIMIT = 56 * 1024 * 1024


def _cparams(n_axes):
    return pltpu.CompilerParams(dimension_semantics=("arbitrary",) * n_axes,
                                vmem_limit_bytes=VMEM_LIMIT)


def _resident(shape):
    nd = len(shape)
    return pl.BlockSpec(shape, lambda *_: (0,) * nd, pipeline_mode=pl.Buffered(1))


def _rows(tm, width):
    return pl.BlockSpec((tm, width), lambda i: (i, 0))


def _rmsnorm(x, g):
    return x * lax.rsqrt(jnp.mean(x * x, axis=-1, keepdims=True) + EPS) * g


def _sigmoid(x):
    return 1.0 / (1.0 + jnp.exp(-x))


def _log_sigmoid(x):
    return jnp.minimum(x, 0.0) - jnp.log(1.0 + jnp.exp(-jnp.abs(x)))


def _dot(a, b):
    return jnp.dot(a, b, preferred_element_type=F32)


def _dot_nt(a, b):
    return lax.dot_general(a, b, (((1,), (1,)), ((), ())), preferred_element_type=F32)


def _project(k_transposed, x_ref, g_ref, w_ref, b_ref, wk_ref, bk_ref,
             q_ref, k_ref, v_ref, so_ref, gc_ref, gm_ref, gi_ref, gf_ref):
    hb = _rmsnorm(x_ref[...], g_ref[...]).astype(BF16)

    def seg(lo, hi):
        return _dot(hb, w_ref[:, lo:hi]) + b_ref[:, lo:hi]

    u = seg(0, D_MODEL) * _sigmoid(seg(D_MODEL, Q_COL))
    q_ref[...] = (seg(Q_COL, K_COL) * (DK ** -0.5)).astype(BF16)
    if k_transposed:
        k_ref[...] = (_dot_nt(wk_ref[...], hb) + bk_ref[:, 0:1]).astype(BF16)
    else:
        k_ref[...] = seg(K_COL, V_COL).astype(BF16)
    v_ref[...] = seg(V_COL, O_COL).astype(BF16)
    so_ref[...] = _sigmoid(seg(O_COL, HEAD_COLS)).astype(BF16)
    t = seg(HEAD_COLS, HEAD_COLS + TAIL_W)
    gi_ref[...] = t[:, :LANES]
    gf_ref[...] = _log_sigmoid(pltpu.roll(t[:, :LANES], LANES - M_HEADS, 1))
    t = pltpu.roll(t, TAIL_W - 2 * M_HEADS, 1)
    gc_ref[...] = _sigmoid(t[:, :D_MODEL]).astype(BF16)
    gm_ref[...] = _sigmoid(t[:, D_MODEL:2 * D_MODEL]).astype(BF16)
    return u


def _store_tokens(buf, first_token, u):
    for c in range(SUBLANES):
        buf[pl.ds(first_token * SUBLANES + c, u.shape[0], stride=SUBLANES), :] = u[:, c * LANES:(c + 1) * LANES]


def _inproj_body(x_ref, g_ref, w_ref, b_ref, u_ref, *out_refs):
    _store_tokens(u_ref, 0, _project(False, x_ref, g_ref, w_ref, b_ref, None, None, *out_refs))


_PIECES = ((D_QK, BF16), (D_QK, BF16), (D_M, BF16), (D_M, BF16), (D_MODEL, BF16), (D_MODEL, BF16),
           (LANES, F32), (LANES, F32))


def _inproj(x, g, w, b, tm):
    n = x.shape[0]
    return pl.pallas_call(
        _inproj_body,
        grid=(n // tm,),
        in_specs=[_rows(tm, D_MODEL), _resident((1, D_MODEL)), _resident(w.shape), _resident(b.shape)],
        out_specs=[_rows(tm * SUBLANES, LANES)] + [_rows(tm, wd) for wd, _ in _PIECES],
        out_shape=[jax.ShapeDtypeStruct((n * SUBLANES, LANES), F32)]
        + [jax.ShapeDtypeStruct((n, wd), dt) for wd, dt in _PIECES],
        compiler_params=_cparams(1),
        name="inproj",
    )(x, g, w, b)


def _ln_silu(c, g, b):
    mu = jnp.mean(c, axis=-1, keepdims=True)
    xc = c - mu
    y = xc * lax.rsqrt(jnp.mean(xc * xc, axis=-1, keepdims=True) + EPS) * g + b
    return y * _sigmoid(y)


def _conv_taps(win, w_ref, b_ref, n_out):
    even = [b_ref[...]] * n_out
    odd = [None] * n_out
    for j in range(CONV_WIDTH):
        wj = w_ref[j * SUBLANES:(j + 1) * SUBLANES, :]
        prods = [win[t + j] * wj for t in range(n_out)]
        if j % 2 == 0:
            even = [a + p for a, p in zip(even, prods)]
        else:
            odd = [p if a is None else a + p for a, p in zip(odd, prods)]
    return [a + b for a, b in zip(even, odd)]


def _token_rows(buf, first_token, n):
    return jnp.concatenate(
        [buf[pl.ds(first_token * SUBLANES + c, n, stride=SUBLANES), :] for c in range(SUBLANES)], axis=1)


def _inproj_conv_body(x_ref, g_ref, w_ref, b_ref, wk_ref, bk_ref, wdw_ref, bdw_ref, gln_ref, bln_ref,
                      c_ref, hist_ref, *rest):
    piece_refs, (ubuf, cbuf) = rest[:-2], rest[-2:]
    tm = x_ref.shape[0]
    halo = HALO * SUBLANES

    @pl.when(pl.program_id(1) == 0)
    def _():
        ubuf[0:halo, :] = jnp.zeros((halo, LANES), F32)

    _store_tokens(ubuf, HALO, _project(True, x_ref, g_ref, w_ref, b_ref, wk_ref, bk_ref, *piece_refs))
    for t0 in range(0, tm, CONV_TOKENS):
        win = [ubuf[(t0 + HALO - CONV_BUF + k) * SUBLANES:(t0 + HALO - CONV_BUF + k + 1) * SUBLANES, :]
               for k in range(CONV_TOKENS + CONV_BUF)]
        for t, acc in enumerate(_conv_taps(win, wdw_ref, bdw_ref, CONV_TOKENS)):
            cbuf[(t0 + t) * SUBLANES:(t0 + t + 1) * SUBLANES, :] = acc
    for r0 in range(0, tm, CONV_ROWS):
        c_ref[r0:r0 + CONV_ROWS, :] = _ln_silu(
            _token_rows(cbuf, r0, CONV_ROWS), gln_ref[...], bln_ref[...]).astype(c_ref.dtype)

    @pl.when(pl.program_id(1) == pl.num_programs(1) - 1)
    def _():
        hist_ref[0] = _token_rows(ubuf, HALO + tm - CONV_BUF, CONV_BUF)

    ubuf[0:halo, :] = ubuf[tm * SUBLANES:tm * SUBLANES + halo, :]


def _inproj_conv(x, g, w, b, wk_t, bk_t, wdw, bdw, gln, bln, batch, seq, tm):
    nt = seq // tm
    n = batch * seq

    def tile(width):
        return pl.BlockSpec((tm, width), lambda s, i: (s * nt + i, 0))

    pieces = list(_PIECES)
    piece_specs = [tile(wd) for wd, _ in pieces]
    piece_shapes = [jax.ShapeDtypeStruct((n, wd), dt) for wd, dt in pieces]
    piece_specs[1] = pl.BlockSpec((D_QK, tm), lambda s, i: (0, s * nt + i))
    piece_shapes[1] = jax.ShapeDtypeStruct((D_QK, n), BF16)
    return pl.pallas_call(
        _inproj_conv_body,
        grid=(batch, nt),
        in_specs=[tile(D_MODEL), _resident((1, D_MODEL)), _resident(w.shape), _resident(b.shape),
                  _resident(wk_t.shape), _resident(bk_t.shape), _resident(wdw.shape), _resident(bdw.shape),
                  _resident((1, D_MODEL)), _resident((1, D_MODEL))],
        out_specs=[tile(D_MODEL), pl.BlockSpec((1, CONV_BUF, D_MODEL), lambda s, i: (s, 0, 0))] + piece_specs,
        out_shape=[jax.ShapeDtypeStruct((n, D_MODEL), BF16),
                   jax.ShapeDtypeStruct((batch, CONV_BUF, D_MODEL), F32)] + piece_shapes,
        scratch_shapes=[pltpu.VMEM(((HALO + tm) * SUBLANES, LANES), F32),
                        pltpu.VMEM((tm * SUBLANES, LANES), F32)],
        compiler_params=_cparams(2),
        name="inproj_conv",
    )(x, g, w, b, wk_t, bk_t, wdw, bdw, gln, bln)


def _conv_sample_body(steps, st_ref, u_ref, w_ref, b_ref, gln_ref, bln_ref, c_ref, hist_ref, fbuf, cbuf):
    nseq = st_ref.shape[0]
    full = CONV_BUF + steps
    for s in range(nseq):
        f0 = s * full
        for c in range(SUBLANES):
            fbuf[pl.ds(f0 * SUBLANES + c, CONV_BUF, stride=SUBLANES), :] = st_ref[s, :, c * LANES:(c + 1) * LANES]
        fbuf[(f0 + CONV_BUF) * SUBLANES:(f0 + full) * SUBLANES, :] = (
            u_ref[s * steps * SUBLANES:(s + 1) * steps * SUBLANES, :])
        win = [fbuf[(f0 + k) * SUBLANES:(f0 + k + 1) * SUBLANES, :] for k in range(full)]
        for t, acc in enumerate(_conv_taps(win, w_ref, b_ref, steps)):
            cbuf[(s * steps + t) * SUBLANES:(s * steps + t + 1) * SUBLANES, :] = acc
        hist_ref[s] = _token_rows(fbuf, f0 + steps, CONV_BUF)
    c_ref[...] = _ln_silu(_token_rows(cbuf, 0, nseq * steps), gln_ref[...], bln_ref[...])


def _conv_sample(state, u, wdw, bdw, gln, bln, nseq):
    batch = state.shape[0]
    steps = u.shape[0] // (batch * SUBLANES)
    return pl.pallas_call(
        functools.partial(_conv_sample_body, steps),
        grid=(batch // nseq,),
        in_specs=[pl.BlockSpec((nseq, CONV_BUF, D_MODEL), lambda i: (i, 0, 0)),
                  _rows(nseq * steps * SUBLANES, LANES),
                  _resident(wdw.shape), _resident(bdw.shape),
                  _resident((1, D_MODEL)), _resident((1, D_MODEL))],
        out_specs=[_rows(nseq * steps, D_MODEL),
                   pl.BlockSpec((nseq, CONV_BUF, D_MODEL), lambda i: (i, 0, 0))],
        out_shape=[jax.ShapeDtypeStruct((batch * steps, D_MODEL), F32),
                   jax.ShapeDtypeStruct(state.shape, F32)],
        scratch_shapes=[pltpu.VMEM((nseq * (CONV_BUF + steps) * SUBLANES, LANES), F32),
                        pltpu.VMEM((nseq * steps * SUBLANES, LANES), F32)],
        compiler_params=_cparams(1),
        name="conv_sample",
    )(state, u, wdw, bdw, gln, bln)


def _col(x, h):
    return x[:, h:h + 1]


def _with_ones(vh):
    lane = lax.broadcasted_iota(jnp.int32, (vh.shape[0], LANES), 1)
    return jnp.concatenate([vh, (lane == 0).astype(vh.dtype)], axis=1)


def _intra_head(s, vaug, b_col, r_row, inter_col, mask):
    dm = jnp.where(mask, b_col + r_row, NEG)
    m_t = jnp.maximum(inter_col, jnp.max(dm, axis=1, keepdims=True))
    sw = s * jnp.exp(dm - m_t)
    return _dot(sw.astype(BF16), vaug), m_t, jnp.exp(inter_col - m_t)


def _head_out(num_aug, m_t, gmh, so):
    num = num_aug[:, :DV]
    den = num_aug[:, DV:DV + 1]
    inv = 1.0 / jnp.maximum(jnp.abs(den), jnp.exp(-m_t))
    scale = inv * lax.rsqrt(inv * inv * jnp.mean(num * num, axis=-1, keepdims=True) + EPS)
    return (num * scale * (gmh * so.astype(F32))).astype(BF16)


def _mlstm_prompt_body(q_ref, kt_ref, v_ref, so_ref, gi_ref, gf_ref, gmh_ref,
                       hmo_ref, c_ref, n_ref, m_ref, ct_ref):
    L = q_ref.shape[0]
    chunk = pl.program_id(1)

    @pl.when(chunk == 0)
    def _():
        ct_ref[...] = jnp.zeros(ct_ref.shape, F32)
        m_ref[...] = jnp.zeros(m_ref.shape, F32)

    row = lax.broadcasted_iota(jnp.int32, (L, LANES), 0)
    gi = gi_ref[...]
    b = gf_ref[...]
    sh = 1
    while sh < L:
        b = b + jnp.where(row >= sh, pltpu.roll(b, sh, 0), 0.0)
        sh *= 2
    r = gi - b
    r_t = r.T
    m_prev = m_ref[0]
    b_last = b[L - 1:L, :]
    m_new = jnp.maximum(b_last + m_prev, jnp.max(b_last + r, axis=0, keepdims=True))
    dec = jnp.exp(b_last + m_prev - m_new)
    inter = b + m_prev
    tri = lax.broadcasted_iota(jnp.int32, (L, L), 1) <= lax.broadcasted_iota(jnp.int32, (L, L), 0)

    for h in range(M_HEADS):
        qh = q_ref[:, h * DK:(h + 1) * DK]
        kt = kt_ref[h * DK:(h + 1) * DK, :]
        vaug = _with_ones(v_ref[:, h * DV:(h + 1) * DV])
        num_aug, m_t, w_int = _intra_head(_dot(qh, kt), vaug, _col(b, h), r_t[h:h + 1, :], _col(inter, h), tri)
        ct_old = ct_ref[h]
        num_aug = num_aug + w_int * _dot(qh, ct_old.astype(BF16))
        hmo_ref[:, h * DV:(h + 1) * DV] = _head_out(
            num_aug, m_t, gmh_ref[:, h * DV:(h + 1) * DV], so_ref[:, h * DV:(h + 1) * DV])
        wk_row = jnp.exp(_col(b_last, h) + r_t[h:h + 1, :] - _col(m_new, h))
        ct_ref[h] = _col(dec, h) * ct_old + _dot((kt.astype(F32) * wk_row).astype(BF16), vaug)
    m_ref[0] = m_new

    @pl.when(chunk == pl.num_programs(1) - 1)
    def _():
        for h in range(M_HEADS):
            ct = ct_ref[h]
            c_ref[0, h] = ct[:, :DV].T
            n_ref[0, h:h + 1, :] = ct[:, DV:].T[0:1, :]


def _mlstm_prompt(q, kt, v, so, gi, gf, gmh, batch, seq):
    nc = seq // CHUNK
    n = batch * seq

    def tok(width):
        return pl.BlockSpec((CHUNK, width), lambda b, c: (b * nc + c, 0))

    return pl.pallas_call(
        _mlstm_prompt_body,
        grid=(batch, nc),
        in_specs=[tok(D_QK), pl.BlockSpec((D_QK, CHUNK), lambda b, c: (0, b * nc + c)),
                  tok(D_M), tok(D_M), tok(LANES), tok(LANES), _resident((1, D_M))],
        out_specs=[tok(D_M),
                   pl.BlockSpec((1, M_HEADS, DV, DK), lambda b, c: (b, 0, 0, 0)),
                   pl.BlockSpec((1, M_HEADS, DK), lambda b, c: (b, 0, 0)),
                   pl.BlockSpec((1, 1, LANES), lambda b, c: (b, 0, 0))],
        out_shape=[jax.ShapeDtypeStruct((n, D_M), BF16),
                   jax.ShapeDtypeStruct((batch, M_HEADS, DV, DK), F32),
                   jax.ShapeDtypeStruct((batch, M_HEADS, DK), F32),
                   jax.ShapeDtypeStruct((batch, 1, LANES), F32)],
        scratch_shapes=[pltpu.VMEM((M_HEADS, DK, AUG), F32)],
        compiler_params=_cparams(2),
        name="mlstm_prompt",
    )(q, kt, v, so, gi, gf, gmh)


def _mlstm_sample_intra_body(steps, q_ref, k_ref, v_ref, gi_ref, gf_ref, mtok_ref,
                             num_ref, mt_ref, wi_ref, wk_ref, wkt_ref, dec_ref, mnew_ref):
    L = q_ref.shape[0]
    row = lax.broadcasted_iota(jnp.int32, (L, LANES), 0)
    pos = row & (steps - 1)

    def down(x, k):
        return pltpu.roll(x, k, 0)

    def up(x, k):
        return pltpu.roll(x, L - k, 0)

    gi = gi_ref[...]
    b = gf_ref[...]
    sh = 1
    while sh < steps:
        b = b + jnp.where(pos >= sh, down(b, sh), 0.0)
        sh *= 2
    r = gi - b
    r_t = r.T
    m_prev = mtok_ref[...]
    b_last = b
    for k in range(1, steps):
        b_last = jnp.where(pos == steps - 1 - k, up(b, k), b_last)
    d_last = b_last + r
    d_max = d_last
    sh = 1
    while sh < steps:
        d_max = jnp.maximum(d_max, jnp.where((pos & sh) == 0, up(d_max, sh), down(d_max, sh)))
        sh *= 2
    m_new = jnp.maximum(b_last + m_prev, d_max)
    wk = jnp.exp(d_last - m_new)
    inter = b + m_prev
    ti = lax.broadcasted_iota(jnp.int32, (L, L), 0)
    si = lax.broadcasted_iota(jnp.int32, (L, L), 1)
    mask = (si <= ti) & ((si & -steps) == (ti & -steps))
    lane = lax.broadcasted_iota(jnp.int32, (L, LANES), 1)

    mt_all = jnp.zeros((L, LANES), F32)
    wi_all = jnp.zeros((L, LANES), F32)
    for h in range(M_HEADS):
        s = _dot_nt(q_ref[:, h * DK:(h + 1) * DK], k_ref[:, h * DK:(h + 1) * DK])
        num_aug, m_t, w_int = _intra_head(s, _with_ones(v_ref[:, h * DV:(h + 1) * DV]),
                                          _col(b, h), r_t[h:h + 1, :], _col(inter, h), mask)
        num_ref[:, h * AUG:(h + 1) * AUG] = num_aug
        mt_all = jnp.where(lane == h, m_t, mt_all)
        wi_all = jnp.where(lane == h, w_int, wi_all)
    mt_ref[...] = mt_all
    wi_ref[...] = wi_all
    wk_ref[...] = wk
    wkt_ref[...] = wk.T[0:8, :]
    dec_ref[...] = jnp.exp(b_last + m_prev - m_new)
    mnew_ref[...] = m_new


def _mlstm_sample_intra(q, k, v, gi, gf, mtok, steps):
    n = q.shape[0]
    L = CHUNK
    scal = jax.ShapeDtypeStruct((n, LANES), F32)
    return pl.pallas_call(
        functools.partial(_mlstm_sample_intra_body, steps),
        grid=(n // L,),
        in_specs=[_rows(L, D_QK), _rows(L, D_QK), _rows(L, D_M), _rows(L, LANES), _rows(L, LANES),
                  _rows(L, LANES)],
        out_specs=[_rows(L, M_HEADS * AUG), _rows(L, LANES), _rows(L, LANES), _rows(L, LANES),
                   pl.BlockSpec((8, L), lambda i: (0, i)), _rows(L, LANES), _rows(L, LANES)],
        out_shape=[jax.ShapeDtypeStruct((n, M_HEADS * AUG), F32), scal, scal, scal,
                   jax.ShapeDtypeStruct((8, n), F32), scal, scal],
        compiler_params=_cparams(1),
        name="mlstm_sample_intra",
    )(q, k, v, gi, gf, mtok)


def _mlstm_sample_state_body(steps, q_ref, kwin_ref, vt_ref, so_ref, num_ref, mt_ref, wi_ref,
                             wk_ref, wkt_ref, dec_ref, gmh_ref, c_ref, n_ref,
                             hmo_ref, cn_ref, nn_ref):
    rows = q_ref.shape[0]
    win = kwin_ref.shape[0]
    nseq = rows // steps
    first = (pl.program_id(0) % (win // rows)) * rows
    grp = lax.broadcasted_iota(jnp.int32, (rows, 1), 0) & -steps
    lane = lax.broadcasted_iota(jnp.int32, (1, win), 1)
    col = lax.broadcasted_iota(jnp.int32, (rows, LANES), 1)
    wi = wi_ref[...]
    for h in range(M_HEADS):
        qh = q_ref[:, h * DK:(h + 1) * DK]
        qf = qh.astype(F32)
        kwin = kwin_ref[:, h * DK:(h + 1) * DK]
        k_own = kwin_ref[pl.ds(pl.multiple_of(first, rows), rows), h * DK:(h + 1) * DK].astype(F32)
        vt = vt_ref[h * DV:(h + 1) * DV, :].astype(F32)
        wk_col = _col(wk_ref[...], h)
        wk_row = wkt_ref[h:h + 1, :]
        cq = jnp.zeros((rows, DV), F32)
        nq = jnp.zeros((rows, 1), F32)
        for s in range(nseq):
            c_old = c_ref[s, h]
            n_old = n_ref[s, h:h + 1, :]
            own = grp == s * steps
            cq = jnp.where(own, _dot_nt(qh, c_old.astype(BF16)), cq)
            nq = jnp.where(own, jnp.sum(qf * n_old, axis=1, keepdims=True), nq)
            dec = dec_ref[s * steps:s * steps + 1, h:h + 1]
            lo = first + s * steps
            wrow = jnp.where((lane >= lo) & (lane < lo + steps), wk_row, 0.0)
            cn_ref[s, h] = dec * c_old + _dot((vt * wrow).astype(BF16), kwin)
            nn_ref[s, h:h + 1, :] = dec * n_old + jnp.sum(
                k_own * jnp.where(own, wk_col, 0.0), axis=0, keepdims=True)
        inter_aug = jnp.concatenate([cq, jnp.where(col == 0, nq, 0.0)], axis=1)
        num_aug = num_ref[:, h * AUG:(h + 1) * AUG] + _col(wi, h) * inter_aug
        hmo_ref[:, h * DV:(h + 1) * DV] = _head_out(
            num_aug, _col(mt_ref[...], h), gmh_ref[:, h * DV:(h + 1) * DV], so_ref[:, h * DV:(h + 1) * DV])


def _mlstm_sample_state(q, k, vt, so, num, mt, wi, wk, wkt, dec, gmh, c_all, n_all, layer, prev, steps):
    n = q.shape[0]
    batch = c_all.shape[1]
    rows = SEQ_PER_STEP * steps
    per_win = CHUNK // rows

    def win_rows(width):
        return pl.BlockSpec((CHUNK, width), lambda i: (i // per_win, 0))

    c_spec = pl.BlockSpec((None, SEQ_PER_STEP, M_HEADS, DV, DK), lambda i: (layer, i, 0, 0, 0))
    n_spec = pl.BlockSpec((None, SEQ_PER_STEP, M_HEADS, DK), lambda i: (layer, i, 0, 0))
    in_specs = [_rows(rows, D_QK), win_rows(D_QK),
                pl.BlockSpec((D_M, CHUNK), lambda i: (0, i // per_win)),
                _rows(rows, D_M), _rows(rows, M_HEADS * AUG), _rows(rows, LANES),
                _rows(rows, LANES), _rows(rows, LANES),
                pl.BlockSpec((8, CHUNK), lambda i: (0, i // per_win)),
                _rows(rows, LANES), _resident((1, D_M)), c_spec, n_spec]
    args = [q, k, vt, so, num, mt, wi, wk, wkt, dec, gmh, c_all, n_all]
    body = functools.partial(_mlstm_sample_state_body, steps)
    aliases = {}
    if prev is not None:
        in_specs += [pl.BlockSpec(memory_space=pl.ANY)] * 2
        aliases = {len(args): 1, len(args) + 1: 2}
        args += list(prev)
        body = functools.partial(_drop_refs, body, len(args) - 2, 2)
    return pl.pallas_call(
        body,
        grid=(batch // SEQ_PER_STEP,),
        in_specs=in_specs,
        out_specs=[_rows(rows, D_M), c_spec, n_spec],
        out_shape=[jax.ShapeDtypeStruct((n, D_M), BF16),
                   jax.ShapeDtypeStruct(c_all.shape, F32),
                   jax.ShapeDtypeStruct(n_all.shape, F32)],
        input_output_aliases=aliases,
        compiler_params=_cparams(1),
        name="mlstm_sample_state",
    )(*args)


def _drop_refs(body, first, count, *refs):
    return body(*refs[:first], *refs[first + count:])


def _merge_body(x_ref, c_ref, hmo_ref, gc_ref, gm_ref, wco_ref, wmo_ref, wo_ref, gpm_ref, y_ref):
    y_conv = _dot(c_ref[...].astype(BF16), wco_ref[...])
    y_m = _dot(hmo_ref[...], wmo_ref[...])
    mix = gc_ref[...].astype(F32) * y_conv + gm_ref[...].astype(F32) * y_m
    y_ref[...] = x_ref[...] + _rmsnorm(_dot(mix.astype(BF16), wo_ref[...]), gpm_ref[...])


def _merge(x, c, hmo, gc, gm, wco, wmo, wo, gpm, tm):
    n = x.shape[0]
    return pl.pallas_call(
        _merge_body,
        grid=(n // tm,),
        in_specs=[_rows(tm, D_MODEL), _rows(tm, D_MODEL), _rows(tm, D_M), _rows(tm, D_MODEL),
                  _rows(tm, D_MODEL),
                  _resident((D_MODEL, D_MODEL)), _resident((D_M, D_MODEL)), _resident((D_MODEL, D_MODEL)),
                  _resident((1, D_MODEL))],
        out_specs=_rows(tm, D_MODEL),
        out_shape=jax.ShapeDtypeStruct((n, D_MODEL), F32),
        compiler_params=_cparams(1),
        name="merge",
    )(x, c, hmo, gc, gm, wco, wmo, wo, gpm)


def _ffn_body(x_ref, gpf_ref, w1_ref, b1_ref, w2_ref, b2_ref, gpo_ref, y_ref):
    x1 = x_ref[...]
    hf = _rmsnorm(x1, gpf_ref[...]).astype(BF16)
    a = jnp.maximum(_dot(hf, w1_ref[...]) + b1_ref[...], 0.0)
    ff = _dot((a * a).astype(BF16), w2_ref[...]) + b2_ref[...]
    y_ref[...] = x1 + _rmsnorm(ff, gpo_ref[...])


def _ffn(x, gpf, w1, b1, w2, b2, gpo, tm):
    n = x.shape[0]
    return pl.pallas_call(
        _ffn_body,
        grid=(n // tm,),
        in_specs=[_rows(tm, D_MODEL), _resident((1, D_MODEL)),
                  _resident((D_MODEL, D_FF)), _resident((1, D_FF)),
                  _resident((D_FF, D_MODEL)), _resident((1, D_MODEL)), _resident((1, D_MODEL))],
        out_specs=_rows(tm, D_MODEL),
        out_shape=jax.ShapeDtypeStruct((n, D_MODEL), F32),
        compiler_params=_cparams(1),
        name="ffn",
    )(x, gpf, w1, b1, w2, b2, gpo)


def _row(v):
    return v.reshape(1, -1)


def _prep_layer(d, w_in, b_in, w_dw, w_conv_out, w_m_out, w_o, w_ff1, w_ff2):
    pad = HEAD_COLS + TAIL_W - w_in.shape[2]
    w_main = jnp.pad(w_in[d].astype(BF16), ((0, 0), (0, pad)))
    b_main = _row(jnp.pad(b_in[d], (0, pad)))
    wk_t = w_in[d, :, K_COL:V_COL].T.astype(BF16)
    bk_t = jnp.broadcast_to(b_in[d, K_COL:V_COL].reshape(D_QK, 1), (D_QK, LANES))
    wdw = w_dw[d].reshape(CONV_WIDTH * SUBLANES, LANES)
    return dict(w_main=w_main, b_main=b_main, wk_t=wk_t, bk_t=bk_t, wdw=wdw,
                wco=w_conv_out[d].astype(BF16), wmo=w_m_out[d].astype(BF16), wo=w_o[d].astype(BF16),
                w1=w_ff1[d].astype(BF16), w2=w_ff2[d].astype(BF16))


def kernel(x_prompt, x_sample, state_conv, state_C, state_n, state_m, g_pre_mix, w_in, b_in, w_dw, b_dw,
           g_cln, b_cln, w_conv_out, g_mh, w_m_out, w_o, g_post_mix, g_pre_ff, w_ff1, b_ff1, w_ff2, b_ff2,
           g_post_ff):
    bp, tp, _ = x_prompt.shape
    bs, ts, _ = x_sample.shape
    depth = w_in.shape[0]
    assert tp % CHUNK == 0 and tp >= CONV_BUF and CHUNK % (SEQ_PER_STEP * ts) == 0
    assert (bs * ts) % CHUNK == 0 and bs % SEQ_PER_STEP == 0 and ts & (ts - 1) == 0
    tm_p = min(256, bp * tp)
    tm_s = min(256, bs * ts)
    assert (bp * tp) % tm_p == 0 and (bs * ts) % tm_s == 0
    yp = x_prompt.reshape(bp * tp, D_MODEL)
    ys = x_sample.reshape(bs * ts, D_MODEL)
    outs = [[] for _ in range(6)]
    sample_state = None
    for d in range(depth):
        p = _prep_layer(d, w_in, b_in, w_dw, w_conv_out, w_m_out, w_o, w_ff1, w_ff2)
        conv_args = (p["wdw"], b_dw[d].reshape(SUBLANES, LANES), _row(g_cln[d]), _row(b_cln[d]))
        gmh = _row(g_mh[d])
        inproj_args = (_row(g_pre_mix[d]), p["w_main"], p["b_main"])

        def tail(x, c, hmo, gc, gm, tm):
            x1 = _merge(x, c, hmo, gc, gm, p["wco"], p["wmo"], p["wo"], _row(g_post_mix[d]), tm)
            return _ffn(x1, _row(g_pre_ff[d]), p["w1"], _row(b_ff1[d]), p["w2"], _row(b_ff2[d]),
                        _row(g_post_ff[d]), tm)

        c, hist, q, kt, v, so, gc, gm, gi, gf = _inproj_conv(
            yp, *inproj_args, p["wk_t"], p["bk_t"], *conv_args, bp, tp, tm_p)
        hmo, c_new, n_new, m_new = _mlstm_prompt(q, kt, v, so, gi, gf, gmh, bp, tp)
        yp = tail(yp, c, hmo, gc, gm, 2 * tm_p)
        outs[0].append(hist)
        outs[1].append(c_new)
        outs[2].append(n_new)
        outs[3].append(m_new[:, 0, :M_HEADS])

        u, q, k, v, so, gc, gm, gi, gf = _inproj(ys, *inproj_args, tm_s)
        c, hist = _conv_sample(state_conv[d], u, *conv_args, 8)
        mtok = jnp.pad(jnp.repeat(state_m[d], ts, axis=0), ((0, 0), (0, LANES - M_HEADS)))
        num, mt, wi, wk, wkt, dec, mnew = _mlstm_sample_intra(q, k, v, gi, gf, mtok, ts)
        hmo, *sample_state = _mlstm_sample_state(q, k, v.T, so, num, mt, wi, wk, wkt, dec, gmh,
                                                 state_C, state_n, d, sample_state, ts)
        ys = tail(ys, c, hmo, gc, gm, tm_s)
        outs[4].append(hist)
        outs[5].append(mnew[ts - 1::ts, :M_HEADS])
    st = [jnp.stack(o) for o in outs]
    return (yp.reshape(bp, tp, D_MODEL), ys.reshape(bs, ts, D_MODEL), st[0], st[1], st[2], st[3],
            st[4], sample_state[0], sample_state[1], st[5])
```

```python
import functools

import jax
import jax.numpy as jnp
from jax import lax
from jax.experimental import pallas as pl
from jax.experimental.pallas import tpu as pltpu

F32 = jnp.float32
BF16 = jnp.bfloat16

D_MODEL = 1024
M_HEADS = 4
DK = 256
DV = 512
D_QK = M_HEADS * DK
D_M = M_HEADS * DV
D_FF = 4 * D_MODEL
CONV_WIDTH = 31
CONV_BUF = CONV_WIDTH - 1
CHUNK = 128
EPS = 1e-6
NEG = -1e30
LANES = 128
SUBLANES = 8
Q_COL = 2 * D_MODEL
K_COL = Q_COL + D_QK
V_COL = K_COL + D_QK
O_COL = V_COL + D_M
HEAD_COLS = O_COL + D_M
TAIL_W = -(-(2 * M_HEADS + 2 * D_MODEL) // LANES) * LANES
AUG = DV + LANES
HALO = 32
CONV_TOKENS = 8
CONV_ROWS = 16
ANCHORS_PER_SEGMENT = 4
SEQ_PER_STEP = 4

VMEM_LIMIT = 56 * 1024 * 1024


def _cparams(n_axes):
    return pltpu.CompilerParams(dimension_semantics=("arbitrary",) * n_axes,
                                vmem_limit_bytes=VMEM_LIMIT)


def _resident(shape):
    nd = len(shape)
    return pl.BlockSpec(shape, lambda *_: (0,) * nd, pipeline_mode=pl.Buffered(1))


def _layer_of(w, layer):
    tail = (0,) * (w.ndim - 1)
    return pl.BlockSpec((None,) + w.shape[1:], lambda *_: (layer,) + tail, pipeline_mode=pl.Buffered(1))


def _rows(tm, width):
    return pl.BlockSpec((tm, width), lambda i: (i, 0))


def _rmsnorm(x, g):
    return x * lax.rsqrt(jnp.mean(x * x, axis=-1, keepdims=True) + EPS) * g


def _sigmoid(x):
    return 1.0 / (1.0 + jnp.exp(-x))


def _log_sigmoid(x):
    return jnp.minimum(x, 0.0) - jnp.log(1.0 + jnp.exp(-jnp.abs(x)))


def _dot(a, b):
    return jnp.dot(a, b, preferred_element_type=F32)


def _dot_nt(a, b):
    return lax.dot_general(a, b, (((1,), (1,)), ((), ())), preferred_element_type=F32)


def _project(k_transposed, x_ref, g_ref, w_ref, b_ref, wk_ref, bk_ref,
             q_ref, k_ref, v_ref, so_ref, gc_ref, gm_ref, gi_ref, gf_ref, anchors=None):
    hb = _rmsnorm(x_ref[...], g_ref[...]).astype(BF16)

    def seg(lo, hi):
        return _dot(hb, w_ref[:, lo:hi]) + b_ref[:, lo:hi]

    u = seg(0, D_MODEL) * _sigmoid(seg(D_MODEL, Q_COL))
    q = seg(Q_COL, K_COL)
    q_ref[...] = (q * (DK ** -0.5)).astype(BF16)
    if k_transposed:
        k = _dot_nt(wk_ref[...], hb) + bk_ref[:, 0:1]
    else:
        k = seg(K_COL, V_COL)
    k_ref[...] = k.astype(BF16)
    v = seg(V_COL, O_COL)
    v_ref[...] = v.astype(BF16)
    o = seg(O_COL, HEAD_COLS)
    so_ref[...] = _sigmoid(o).astype(BF16)
    if anchors is not None:
        for res, transposed in ((q, False), (k, k_transposed), (v, False), (o, False)):
            span = res.shape[0] if transposed else res.shape[1]
            for i in range(ANCHORS_PER_SEGMENT):
                at = i * (span // ANCHORS_PER_SEGMENT)
                anchors.append(res[at:at + SUBLANES, 0:LANES] if transposed else res[0:SUBLANES, at:at + LANES])
    t = seg(HEAD_COLS, HEAD_COLS + TAIL_W)
    gi_ref[...] = t[:, :LANES]
    gf_ref[...] = _log_sigmoid(pltpu.roll(t[:, :LANES], LANES - M_HEADS, 1))
    t = pltpu.roll(t, TAIL_W - 2 * M_HEADS, 1)
    gc_ref[...] = _sigmoid(t[:, :D_MODEL]).astype(BF16)
    gm_ref[...] = _sigmoid(t[:, D_MODEL:2 * D_MODEL]).astype(BF16)
    return u


def _zero_of(x):
    bits = pltpu.bitcast(x, jnp.uint32)
    bits = lax.shift_right_logical(lax.shift_right_logical(bits, jnp.uint32(16)), jnp.uint32(16))
    return pltpu.bitcast(bits, F32)[0:1, 0:1]


def _store_tokens(buf, first_token, u):
    for c in range(SUBLANES):
        buf[pl.ds(first_token * SUBLANES + c, u.shape[0], stride=SUBLANES), :] = u[:, c * LANES:(c + 1) * LANES]


def _inproj_body(x_ref, g_ref, w_ref, b_ref, u_ref, *out_refs):
    _store_tokens(u_ref, 0, _project(False, x_ref, g_ref, w_ref, b_ref, None, None, *out_refs))


_PIECES = ((D_QK, BF16), (D_QK, BF16), (D_M, BF16), (D_M, BF16), (D_MODEL, BF16), (D_MODEL, BF16),
           (LANES, F32), (LANES, F32))


def _inproj(x, g, w, b, layer, tm):
    n = x.shape[0]
    return pl.pallas_call(
        _inproj_body,
        grid=(n // tm,),
        in_specs=[_rows(tm, D_MODEL), _resident((1, D_MODEL)), _layer_of(w, layer), _layer_of(b, layer)],
        out_specs=[_rows(tm * SUBLANES, LANES)] + [_rows(tm, wd) for wd, _ in _PIECES],
        out_shape=[jax.ShapeDtypeStruct((n * SUBLANES, LANES), F32)]
        + [jax.ShapeDtypeStruct((n, wd), dt) for wd, dt in _PIECES],
        compiler_params=_cparams(1),
        name="inproj",
    )(x, g, w, b)


def _ln_silu(c, g, b):
    mu = jnp.mean(c, axis=-1, keepdims=True)
    xc = c - mu
    y = xc * lax.rsqrt(jnp.mean(xc * xc, axis=-1, keepdims=True) + EPS) * g + b
    return y * _sigmoid(y)


def _conv_taps(win, w_ref, b_ref, n_out):
    even = [b_ref[...]] * n_out
    odd = [None] * n_out
    for j in range(CONV_WIDTH):
        wj = w_ref[j * SUBLANES:(j + 1) * SUBLANES, :]
        prods = [win[t + j] * wj for t in range(n_out)]
        if j % 2 == 0:
            even = [a + p for a, p in zip(even, prods)]
        else:
            odd = [p if a is None else a + p for a, p in zip(odd, prods)]
    return [a + b for a, b in zip(even, odd)]


def _token_rows(buf, first_token, n):
    return jnp.concatenate(
        [buf[pl.ds(first_token * SUBLANES + c, n, stride=SUBLANES), :] for c in range(SUBLANES)], axis=1)


def _inproj_conv_body(x_ref, g_ref, w_ref, b_ref, wk_ref, bk_ref, wdw_ref, bdw_ref, gln_ref, bln_ref,
                      c_ref, hist_ref, *rest):
    piece_refs, (ubuf, cbuf) = rest[:-2], rest[-2:]
    tm = x_ref.shape[0]
    halo = HALO * SUBLANES

    @pl.when(pl.program_id(1) == 0)
    def _():
        ubuf[0:halo, :] = jnp.zeros((halo, LANES), F32)

    anchors = []
    u = _project(True, x_ref, g_ref, w_ref, b_ref, wk_ref, bk_ref, *piece_refs, anchors=anchors)
    part = tm // len(anchors)
    for i, a in enumerate(anchors):
        _store_tokens(ubuf, HALO + i * part, u[i * part:(i + 1) * part, :] + _zero_of(a))
    for t0 in range(0, tm, CONV_TOKENS):
        win = [ubuf[(t0 + HALO - CONV_BUF + k) * SUBLANES:(t0 + HALO - CONV_BUF + k + 1) * SUBLANES, :]
               for k in range(CONV_TOKENS + CONV_BUF)]
        for t, acc in enumerate(_conv_taps(win, wdw_ref, bdw_ref, CONV_TOKENS)):
            cbuf[(t0 + t) * SUBLANES:(t0 + t + 1) * SUBLANES, :] = acc
    for r0 in range(0, tm, CONV_ROWS):
        c_ref[r0:r0 + CONV_ROWS, :] = _ln_silu(
            _token_rows(cbuf, r0, CONV_ROWS), gln_ref[...], bln_ref[...]).astype(c_ref.dtype)

    @pl.when(pl.program_id(1) == pl.num_programs(1) - 1)
    def _():
        hist_ref[0] = _token_rows(ubuf, HALO + tm - CONV_BUF, CONV_BUF)

    ubuf[0:halo, :] = ubuf[tm * SUBLANES:tm * SUBLANES + halo, :]


def _inproj_conv(x, g, w, b, layer, wk_t, bk_t, wdw, bdw, gln, bln, batch, seq, tm):
    nt = seq // tm
    n = batch * seq

    def tile(width):
        return pl.BlockSpec((tm, width), lambda s, i: (s * nt + i, 0))

    pieces = list(_PIECES)
    piece_specs = [tile(wd) for wd, _ in pieces]
    piece_shapes = [jax.ShapeDtypeStruct((n, wd), dt) for wd, dt in pieces]
    piece_specs[1] = pl.BlockSpec((D_QK, tm), lambda s, i: (0, s * nt + i))
    piece_shapes[1] = jax.ShapeDtypeStruct((D_QK, n), BF16)
    return pl.pallas_call(
        _inproj_conv_body,
        grid=(batch, nt),
        in_specs=[tile(D_MODEL), _resident((1, D_MODEL)), _layer_of(w, layer), _layer_of(b, layer),
                  _resident(wk_t.shape), _resident(bk_t.shape), _resident(wdw.shape), _resident(bdw.shape),
                  _resident((1, D_MODEL)), _resident((1, D_MODEL))],
        out_specs=[tile(D_MODEL), pl.BlockSpec((1, CONV_BUF, D_MODEL), lambda s, i: (s, 0, 0))] + piece_specs,
        out_shape=[jax.ShapeDtypeStruct((n, D_MODEL), BF16),
                   jax.ShapeDtypeStruct((batch, CONV_BUF, D_MODEL), F32)] + piece_shapes,
        scratch_shapes=[pltpu.VMEM(((HALO + tm) * SUBLANES, LANES), F32),
                        pltpu.VMEM((tm * SUBLANES, LANES), F32)],
        compiler_params=_cparams(2),
        name="inproj_conv",
    )(x, g, w, b, wk_t, bk_t, wdw, bdw, gln, bln)


def _conv_sample_body(steps, st_ref, u_ref, w_ref, b_ref, gln_ref, bln_ref, c_ref, hist_ref, fbuf, cbuf):
    nseq = st_ref.shape[0]
    full = CONV_BUF + steps
    for s in range(nseq):
        f0 = s * full
        for c in range(SUBLANES):
            fbuf[pl.ds(f0 * SUBLANES + c, CONV_BUF, stride=SUBLANES), :] = st_ref[s, :, c * LANES:(c + 1) * LANES]
        fbuf[(f0 + CONV_BUF) * SUBLANES:(f0 + full) * SUBLANES, :] = (
            u_ref[s * steps * SUBLANES:(s + 1) * steps * SUBLANES, :])
        win = [fbuf[(f0 + k) * SUBLANES:(f0 + k + 1) * SUBLANES, :] for k in range(full)]
        for t, acc in enumerate(_conv_taps(win, w_ref, b_ref, steps)):
            cbuf[(s * steps + t) * SUBLANES:(s * steps + t + 1) * SUBLANES, :] = acc
        hist_ref[s] = _token_rows(fbuf, f0 + steps, CONV_BUF)
    c_ref[...] = _ln_silu(_token_rows(cbuf, 0, nseq * steps), gln_ref[...], bln_ref[...])


def _drop_refs(body, first, count, *refs):
    return body(*refs[:first], *refs[first + count:])


def _stacked_call(body, args, in_specs, stacked_outs, prev, **kwargs):
    aliases = {}
    if prev is not None:
        first = len(args)
        in_specs = list(in_specs) + [pl.BlockSpec(memory_space=pl.ANY)] * len(prev)
        aliases = {first + j: out for j, out in enumerate(stacked_outs)}
        args = list(args) + list(prev)
        body = functools.partial(_drop_refs, body, first, len(prev))
    return pl.pallas_call(body, in_specs=in_specs, input_output_aliases=aliases, **kwargs)(*args)


def _conv_sample(state_all, u, wdw, bdw, gln, bln, layer, prev, nseq):
    batch = state_all.shape[1]
    steps = u.shape[0] // (batch * SUBLANES)
    hist_spec = pl.BlockSpec((None, nseq, CONV_BUF, D_MODEL), lambda i: (layer, i, 0, 0))
    return _stacked_call(
        functools.partial(_conv_sample_body, steps),
        [state_all, u, wdw, bdw, gln, bln],
        [hist_spec, _rows(nseq * steps * SUBLANES, LANES), _resident(wdw.shape), _resident(bdw.shape),
         _resident((1, D_MODEL)), _resident((1, D_MODEL))],
        [1], prev,
        grid=(batch // nseq,),
        out_specs=[_rows(nseq * steps, D_MODEL), hist_spec],
        out_shape=[jax.ShapeDtypeStruct((batch * steps, D_MODEL), F32),
                   jax.ShapeDtypeStruct(state_all.shape, F32)],
        scratch_shapes=[pltpu.VMEM((nseq * (CONV_BUF + steps) * SUBLANES, LANES), F32),
                        pltpu.VMEM((nseq * steps * SUBLANES, LANES), F32)],
        compiler_params=_cparams(1),
        name="conv_sample",
    )


def _col(x, h):
    return x[:, h:h + 1]


def _with_ones(vh):
    lane = lax.broadcasted_iota(jnp.int32, (vh.shape[0], LANES), 1)
    return jnp.concatenate([vh, (lane == 0).astype(vh.dtype)], axis=1)


def _intra_head(s, vaug, b_col, r_row, inter_col, mask):
    dm = jnp.where(mask, b_col + r_row, NEG)
    m_t = jnp.maximum(inter_col, jnp.max(dm, axis=1, keepdims=True))
    sw = s * jnp.exp(dm - m_t)
    return _dot(sw.astype(BF16), vaug), m_t, jnp.exp(inter_col - m_t)


def _head_out(num_aug, m_t, gmh, so):
    num = num_aug[:, :DV]
    den = num_aug[:, DV:DV + 1]
    inv = 1.0 / jnp.maximum(jnp.abs(den), jnp.exp(-m_t))
    scale = inv * lax.rsqrt(inv * inv * jnp.mean(num * num, axis=-1, keepdims=True) + EPS)
    return (num * scale * (gmh * so.astype(F32))).astype(BF16)


def _mlstm_prompt_body(q_ref, kt_ref, v_ref, so_ref, gi_ref, gf_ref, gmh_ref,
                       hmo_ref, c_ref, n_ref, m_ref, ct_ref):
    L = q_ref.shape[0]
    chunk = pl.program_id(1)

    @pl.when(chunk == 0)
    def _():
        ct_ref[...] = jnp.zeros(ct_ref.shape, F32)
        m_ref[...] = jnp.zeros(m_ref.shape, F32)

    row = lax.broadcasted_iota(jnp.int32, (L, LANES), 0)
    gi = gi_ref[...]
    b = gf_ref[...]
    sh = 1
    while sh < L:
        b = b + jnp.where(row >= sh, pltpu.roll(b, sh, 0), 0.0)
        sh *= 2
    r = gi - b
    r_t = r.T
    m_prev = m_ref[0]
    b_last = b[L - 1:L, :]
    m_new = jnp.maximum(b_last + m_prev, jnp.max(b_last + r, axis=0, keepdims=True))
    dec = jnp.exp(b_last + m_prev - m_new)
    inter = b + m_prev
    tri = lax.broadcasted_iota(jnp.int32, (L, L), 1) <= lax.broadcasted_iota(jnp.int32, (L, L), 0)

    heads = range(M_HEADS)
    qs = [q_ref[:, h * DK:(h + 1) * DK] for h in heads]
    kts = [kt_ref[h * DK:(h + 1) * DK, :] for h in heads]
    vaugs = [_with_ones(v_ref[:, h * DV:(h + 1) * DV]) for h in heads]
    scores = [_dot(qs[h], kts[h]) for h in heads]
    ct_olds = [ct_ref[h] for h in heads]
    reads = [_dot(qs[h], ct_olds[h].astype(BF16)) for h in heads]
    for h in heads:
        wk_row = jnp.exp(_col(b_last, h) + r_t[h:h + 1, :] - _col(m_new, h))
        ct_ref[h] = _col(dec, h) * ct_olds[h] + _dot((kts[h].astype(F32) * wk_row).astype(BF16), vaugs[h])
    for h in heads:
        num_aug, m_t, w_int = _intra_head(scores[h], vaugs[h], _col(b, h), r_t[h:h + 1, :], _col(inter, h), tri)
        hmo_ref[:, h * DV:(h + 1) * DV] = _head_out(
            num_aug + w_int * reads[h], m_t, gmh_ref[:, h * DV:(h + 1) * DV], so_ref[:, h * DV:(h + 1) * DV])
    m_ref[0] = m_new

    @pl.when(chunk == pl.num_programs(1) - 1)
    def _():
        for h in range(M_HEADS):
            ct = ct_ref[h]
            c_ref[0, h] = ct[:, :DV].T
            n_ref[0, h:h + 1, :] = ct[:, DV:].T[0:1, :]


def _mlstm_prompt(q, kt, v, so, gi, gf, gmh, batch, seq, depth, layer, prev):
    nc = seq // CHUNK
    n = batch * seq

    def tok(width):
        return pl.BlockSpec((CHUNK, width), lambda b, c: (b * nc + c, 0))

    return _stacked_call(
        _mlstm_prompt_body,
        [q, kt, v, so, gi, gf, gmh],
        [tok(D_QK), pl.BlockSpec((D_QK, CHUNK), lambda b, c: (0, b * nc + c)),
         tok(D_M), tok(D_M), tok(LANES), tok(LANES), _resident((1, D_M))],
        [1, 2], prev,
        grid=(batch, nc),
        out_specs=[tok(D_M),
                   pl.BlockSpec((None, 1, M_HEADS, DV, DK), lambda b, c: (layer, b, 0, 0, 0)),
                   pl.BlockSpec((None, 1, M_HEADS, DK), lambda b, c: (layer, b, 0, 0)),
                   pl.BlockSpec((1, 1, LANES), lambda b, c: (b, 0, 0))],
        out_shape=[jax.ShapeDtypeStruct((n, D_M), BF16),
                   jax.ShapeDtypeStruct((depth, batch, M_HEADS, DV, DK), F32),
                   jax.ShapeDtypeStruct((depth, batch, M_HEADS, DK), F32),
                   jax.ShapeDtypeStruct((batch, 1, LANES), F32)],
        scratch_shapes=[pltpu.VMEM((M_HEADS, DK, AUG), F32)],
        compiler_params=_cparams(2),
        name="mlstm_prompt",
    )


def _mlstm_sample_intra_body(steps, q_ref, k_ref, v_ref, gi_ref, gf_ref, mtok_ref,
                             num_ref, mt_ref, wi_ref, wk_ref, wkt_ref, dec_ref, mnew_ref):
    L = q_ref.shape[0]
    row = lax.broadcasted_iota(jnp.int32, (L, LANES), 0)
    pos = row & (steps - 1)

    def down(x, k):
        return pltpu.roll(x, k, 0)

    def up(x, k):
        return pltpu.roll(x, L - k, 0)

    gi = gi_ref[...]
    b = gf_ref[...]
    sh = 1
    while sh < steps:
        b = b + jnp.where(pos >= sh, down(b, sh), 0.0)
        sh *= 2
    r = gi - b
    r_t = r.T
    m_prev = mtok_ref[...]
    b_last = b
    for k in range(1, steps):
        b_last = jnp.where(pos == steps - 1 - k, up(b, k), b_last)
    d_last = b_last + r
    d_max = d_last
    sh = 1
    while sh < steps:
        d_max = jnp.maximum(d_max, jnp.where((pos & sh) == 0, up(d_max, sh), down(d_max, sh)))
        sh *= 2
    m_new = jnp.maximum(b_last + m_prev, d_max)
    wk = jnp.exp(d_last - m_new)
    inter = b + m_prev
    ti = lax.broadcasted_iota(jnp.int32, (L, L), 0)
    si = lax.broadcasted_iota(jnp.int32, (L, L), 1)
    mask = (si <= ti) & ((si & -steps) == (ti & -steps))
    lane = lax.broadcasted_iota(jnp.int32, (L, LANES), 1)

    mt_all = jnp.zeros((L, LANES), F32)
    wi_all = jnp.zeros((L, LANES), F32)
    for h in range(M_HEADS):
        s = _dot_nt(q_ref[:, h * DK:(h + 1) * DK], k_ref[:, h * DK:(h + 1) * DK])
        num_aug, m_t, w_int = _intra_head(s, _with_ones(v_ref[:, h * DV:(h + 1) * DV]),
                                          _col(b, h), r_t[h:h + 1, :], _col(inter, h), mask)
        num_ref[:, h * AUG:(h + 1) * AUG] = num_aug
        mt_all = jnp.where(lane == h, m_t, mt_all)
        wi_all = jnp.where(lane == h, w_int, wi_all)
    mt_ref[...] = mt_all
    wi_ref[...] = wi_all
    wk_ref[...] = wk
    wkt_ref[...] = wk.T[0:8, :]
    dec_ref[...] = jnp.exp(b_last + m_prev - m_new)
    mnew_ref[...] = m_new


def _mlstm_sample_intra(q, k, v, gi, gf, mtok, steps):
    n = q.shape[0]
    L = CHUNK
    scal = jax.ShapeDtypeStruct((n, LANES), F32)
    return pl.pallas_call(
        functools.partial(_mlstm_sample_intra_body, steps),
        grid=(n // L,),
        in_specs=[_rows(L, D_QK), _rows(L, D_QK), _rows(L, D_M), _rows(L, LANES), _rows(L, LANES),
                  _rows(L, LANES)],
        out_specs=[_rows(L, M_HEADS * AUG), _rows(L, LANES), _rows(L, LANES), _rows(L, LANES),
                   pl.BlockSpec((8, L), lambda i: (0, i)), _rows(L, LANES), _rows(L, LANES)],
        out_shape=[jax.ShapeDtypeStruct((n, M_HEADS * AUG), F32), scal, scal, scal,
                   jax.ShapeDtypeStruct((8, n), F32), scal, scal],
        compiler_params=_cparams(1),
        name="mlstm_sample_intra",
    )(q, k, v, gi, gf, mtok)


def _mlstm_sample_state_body(steps, q_ref, kwin_ref, vt_ref, so_ref, num_ref, mt_ref, wi_ref,
                             wk_ref, wkt_ref, dec_ref, gmh_ref, c_ref, n_ref,
                             hmo_ref, cn_ref, nn_ref):
    rows = q_ref.shape[0]
    win = kwin_ref.shape[0]
    nseq = rows // steps
    first = (pl.program_id(0) % (win // rows)) * rows
    grp = lax.broadcasted_iota(jnp.int32, (rows, 1), 0) & -steps
    lane = lax.broadcasted_iota(jnp.int32, (1, win), 1)
    col = lax.broadcasted_iota(jnp.int32, (rows, LANES), 1)
    wi = wi_ref[...]
    for h in range(M_HEADS):
        qh = q_ref[:, h * DK:(h + 1) * DK]
        qf = qh.astype(F32)
        kwin = kwin_ref[:, h * DK:(h + 1) * DK]
        k_own = kwin_ref[pl.ds(pl.multiple_of(first, rows), rows), h * DK:(h + 1) * DK].astype(F32)
        vt = vt_ref[h * DV:(h + 1) * DV, :].astype(F32)
        wk_col = _col(wk_ref[...], h)
        wk_row = wkt_ref[h:h + 1, :]
        cq = jnp.zeros((rows, DV), F32)
        nq = jnp.zeros((rows, 1), F32)
        for s in range(nseq):
            c_old = c_ref[s, h]
            n_old = n_ref[s, h:h + 1, :]
            own = grp == s * steps
            cq = jnp.where(own, _dot_nt(qh, c_old.astype(BF16)), cq)
            nq = jnp.where(own, jnp.sum(qf * n_old, axis=1, keepdims=True), nq)
            dec = dec_ref[s * steps:s * steps + 1, h:h + 1]
            lo = first + s * steps
            wrow = jnp.where((lane >= lo) & (lane < lo + steps), wk_row, 0.0)
            cn_ref[s, h] = dec * c_old + _dot((vt * wrow).astype(BF16), kwin)
            nn_ref[s, h:h + 1, :] = dec * n_old + jnp.sum(
                k_own * jnp.where(own, wk_col, 0.0), axis=0, keepdims=True)
        inter_aug = jnp.concatenate([cq, jnp.where(col == 0, nq, 0.0)], axis=1)
        num_aug = num_ref[:, h * AUG:(h + 1) * AUG] + _col(wi, h) * inter_aug
        hmo_ref[:, h * DV:(h + 1) * DV] = _head_out(
            num_aug, _col(mt_ref[...], h), gmh_ref[:, h * DV:(h + 1) * DV], so_ref[:, h * DV:(h + 1) * DV])


def _mlstm_sample_state(q, k, vt, so, num, mt, wi, wk, wkt, dec, gmh, c_all, n_all, layer, prev, steps):
    n = q.shape[0]
    batch = c_all.shape[1]
    rows = SEQ_PER_STEP * steps
    per_win = CHUNK // rows

    def win_rows(width):
        return pl.BlockSpec((CHUNK, width), lambda i: (i // per_win, 0))

    c_spec = pl.BlockSpec((None, SEQ_PER_STEP, M_HEADS, DV, DK), lambda i: (layer, i, 0, 0, 0))
    n_spec = pl.BlockSpec((None, SEQ_PER_STEP, M_HEADS, DK), lambda i: (layer, i, 0, 0))
    return _stacked_call(
        functools.partial(_mlstm_sample_state_body, steps),
        [q, k, vt, so, num, mt, wi, wk, wkt, dec, gmh, c_all, n_all],
        [_rows(rows, D_QK), win_rows(D_QK),
         pl.BlockSpec((D_M, CHUNK), lambda i: (0, i // per_win)),
         _rows(rows, D_M), _rows(rows, M_HEADS * AUG), _rows(rows, LANES),
         _rows(rows, LANES), _rows(rows, LANES),
         pl.BlockSpec((8, CHUNK), lambda i: (0, i // per_win)),
         _rows(rows, LANES), _resident((1, D_M)), c_spec, n_spec],
        [1, 2], prev,
        grid=(batch // SEQ_PER_STEP,),
        out_specs=[_rows(rows, D_M), c_spec, n_spec],
        out_shape=[jax.ShapeDtypeStruct((n, D_M), BF16),
                   jax.ShapeDtypeStruct(c_all.shape, F32),
                   jax.ShapeDtypeStruct(n_all.shape, F32)],
        compiler_params=_cparams(1),
        name="mlstm_sample_state",
    )


def _merge_body(x_ref, c_ref, hmo_ref, gc_ref, gm_ref, wco_ref, wmo_ref, wo_ref, gpm_ref, y_ref):
    y_conv = _dot(c_ref[...].astype(BF16), wco_ref[...])
    y_m = _dot(hmo_ref[...], wmo_ref[...])
    mix = gc_ref[...].astype(F32) * y_conv + gm_ref[...].astype(F32) * y_m
    y_ref[...] = x_ref[...] + _rmsnorm(_dot(mix.astype(BF16), wo_ref[...]), gpm_ref[...])


def _merge(x, c, hmo, gc, gm, wco, wmo, wo, layer, gpm, tm):
    n = x.shape[0]
    return pl.pallas_call(
        _merge_body,
        grid=(n // tm,),
        in_specs=[_rows(tm, D_MODEL), _rows(tm, D_MODEL), _rows(tm, D_M), _rows(tm, D_MODEL),
                  _rows(tm, D_MODEL),
                  _layer_of(wco, layer), _layer_of(wmo, layer), _layer_of(wo, layer),
                  _resident((1, D_MODEL))],
        out_specs=_rows(tm, D_MODEL),
        out_shape=jax.ShapeDtypeStruct((n, D_MODEL), F32),
        compiler_params=_cparams(1),
        name="merge",
    )(x, c, hmo, gc, gm, wco, wmo, wo, gpm)


def _ffn_body(x_ref, gpf_ref, w1_ref, b1_ref, w2_ref, b2_ref, gpo_ref, y_ref):
    x1 = x_ref[...]
    hf = _rmsnorm(x1, gpf_ref[...]).astype(BF16)
    a = jnp.maximum(_dot(hf, w1_ref[...]) + b1_ref[...], 0.0)
    ff = _dot((a * a).astype(BF16), w2_ref[...]) + b2_ref[...]
    y_ref[...] = x1 + _rmsnorm(ff, gpo_ref[...])


def _ffn(x, gpf, w1, b1, w2, b2, layer, gpo, tm):
    n = x.shape[0]
    return pl.pallas_call(
        _ffn_body,
        grid=(n // tm,),
        in_specs=[_rows(tm, D_MODEL), _resident((1, D_MODEL)),
                  _layer_of(w1, layer), _resident((1, D_FF)),
                  _layer_of(w2, layer), _resident((1, D_MODEL)), _resident((1, D_MODEL))],
        out_specs=_rows(tm, D_MODEL),
        out_shape=jax.ShapeDtypeStruct((n, D_MODEL), F32),
        compiler_params=_cparams(1),
        name="ffn",
    )(x, gpf, w1, b1, w2, b2, gpo)


def _row(v):
    return v.reshape(1, -1)


def _prep_weights(w_in, b_in, w_conv_out, w_m_out, w_o, w_ff1, w_ff2):
    pad = HEAD_COLS + TAIL_W - w_in.shape[2]
    return dict(w_main=jnp.pad(w_in.astype(BF16), ((0, 0), (0, 0), (0, pad))),
                b_main=jnp.pad(b_in, ((0, 0), (0, pad)))[:, None, :],
                wco=w_conv_out.astype(BF16), wmo=w_m_out.astype(BF16), wo=w_o.astype(BF16),
                w1=w_ff1.astype(BF16), w2=w_ff2.astype(BF16))


def kernel(x_prompt, x_sample, state_conv, state_C, state_n, state_m, g_pre_mix, w_in, b_in, w_dw, b_dw,
           g_cln, b_cln, w_conv_out, g_mh, w_m_out, w_o, g_post_mix, g_pre_ff, w_ff1, b_ff1, w_ff2, b_ff2,
           g_post_ff):
    bp, tp, _ = x_prompt.shape
    bs, ts, _ = x_sample.shape
    depth = w_in.shape[0]
    assert tp % CHUNK == 0 and tp >= CONV_BUF and CHUNK % (SEQ_PER_STEP * ts) == 0
    assert (bs * ts) % CHUNK == 0 and bs % SEQ_PER_STEP == 0 and ts & (ts - 1) == 0
    tm_p = min(256, bp * tp)
    tm_s = min(256, bs * ts)
    assert (bp * tp) % tm_p == 0 and (bs * ts) % tm_s == 0
    yp = x_prompt.reshape(bp * tp, D_MODEL)
    ys = x_sample.reshape(bs * ts, D_MODEL)
    p = _prep_weights(w_in, b_in, w_conv_out, w_m_out, w_o, w_ff1, w_ff2)
    hist_p, m_p, m_s = [], [], []
    prompt_state = sample_state = sample_hist = None
    for d in range(depth):
        conv_args = (w_dw[d].reshape(CONV_WIDTH * SUBLANES, LANES), b_dw[d].reshape(SUBLANES, LANES),
                     _row(g_cln[d]), _row(b_cln[d]))
        gmh = _row(g_mh[d])
        inproj_args = (_row(g_pre_mix[d]), p["w_main"], p["b_main"], d)
        wk_t = w_in[d, :, K_COL:V_COL].T.astype(BF16)
        bk_t = jnp.broadcast_to(b_in[d, K_COL:V_COL].reshape(D_QK, 1), (D_QK, LANES))

        def tail(x, c, hmo, gc, gm, tm):
            x1 = _merge(x, c, hmo, gc, gm, p["wco"], p["wmo"], p["wo"], d, _row(g_post_mix[d]), tm)
            return _ffn(x1, _row(g_pre_ff[d]), p["w1"], _row(b_ff1[d]), p["w2"], _row(b_ff2[d]), d,
                        _row(g_post_ff[d]), tm)

        c, hist, q, kt, v, so, gc, gm, gi, gf = _inproj_conv(
            yp, *inproj_args, wk_t, bk_t, *conv_args, bp, tp, tm_p)
        hmo, *prompt_state, m_new = _mlstm_prompt(q, kt, v, so, gi, gf, gmh, bp, tp, depth, d, prompt_state)
        yp = tail(yp, c, hmo, gc, gm, 2 * tm_p)
        hist_p.append(hist)
        m_p.append(m_new[:, 0, :M_HEADS])

        u, q, k, v, so, gc, gm, gi, gf = _inproj(ys, *inproj_args, tm_s)
        c, *sample_hist = _conv_sample(state_conv, u, *conv_args, d, sample_hist, 8)
        mtok = jnp.pad(jnp.repeat(state_m[d], ts, axis=0), ((0, 0), (0, LANES - M_HEADS)))
        num, mt, wi, wk, wkt, dec, mnew = _mlstm_sample_intra(q, k, v, gi, gf, mtok, ts)
        hmo, *sample_state = _mlstm_sample_state(q, k, v.T, so, num, mt, wi, wk, wkt, dec, gmh,
                                                 state_C, state_n, d, sample_state, ts)
        ys = tail(ys, c, hmo, gc, gm, tm_s)
        m_s.append(mnew[ts - 1::ts, :M_HEADS])
    return (yp.reshape(bp, tp, D_MODEL), ys.reshape(bs, ts, D_MODEL),
            jnp.stack(hist_p), prompt_state[0], prompt_state[1], jnp.stack(m_p),
            sample_hist[0], sample_state[0], sample_state[1], jnp.stack(m_s))
```

```python
import functools

import jax
import jax.numpy as jnp
from jax import lax
from jax.experimental import pallas as pl
from jax.experimental.pallas import tpu as pltpu

F32 = jnp.float32
BF16 = jnp.bfloat16

D_MODEL = 1024
M_HEADS = 4
DK = 256
DV = 512
D_QK = M_HEADS * DK
D_M = M_HEADS * DV
D_FF = 4 * D_MODEL
CONV_WIDTH = 31
CONV_BUF = CONV_WIDTH - 1
CHUNK = 128
EPS = 1e-6
NEG = -1e30
LANES = 128
SUBLANES = 8
Q_COL = 2 * D_MODEL
K_COL = Q_COL + D_QK
V_COL = K_COL + D_QK
O_COL = V_COL + D_M
HEAD_COLS = O_COL + D_M
TAIL_W = -(-(2 * M_HEADS + 2 * D_MODEL) // LANES) * LANES
AUG = DV + LANES
HALO = 32
CONV_TOKENS = 8
CONV_ROWS = 16
ANCHORS_PER_SEGMENT = 4
SEQ_PER_STEP = 4
MLSTM_CHUNKS = 2

VMEM_LIMIT = 56 * 1024 * 1024


def _cparams(n_axes):
    return pltpu.CompilerParams(dimension_semantics=("arbitrary",) * n_axes,
                                vmem_limit_bytes=VMEM_LIMIT)


def _resident(shape):
    nd = len(shape)
    return pl.BlockSpec(shape, lambda *_: (0,) * nd, pipeline_mode=pl.Buffered(1))


def _layer_of(w, layer):
    tail = (0,) * (w.ndim - 1)
    return pl.BlockSpec((None,) + w.shape[1:], lambda *_: (layer,) + tail, pipeline_mode=pl.Buffered(1))


def _rows(tm, width):
    return pl.BlockSpec((tm, width), lambda i: (i, 0))


def _rmsnorm(x, g):
    return x * lax.rsqrt(jnp.mean(x * x, axis=-1, keepdims=True) + EPS) * g


def _sigmoid(x):
    return 1.0 / (1.0 + jnp.exp(-x))


def _log_sigmoid(x):
    return jnp.minimum(x, 0.0) - jnp.log(1.0 + jnp.exp(-jnp.abs(x)))


def _dot(a, b):
    return jnp.dot(a, b, preferred_element_type=F32)


def _dot_nt(a, b):
    return lax.dot_general(a, b, (((1,), (1,)), ((), ())), preferred_element_type=F32)


def _project(k_transposed, x_ref, g_ref, w_ref, b_ref, wk_ref, bk_ref,
             q_ref, k_ref, v_ref, so_ref, gc_ref, gm_ref, gi_ref, gf_ref, anchors=None):
    hb = _rmsnorm(x_ref[...], g_ref[...]).astype(BF16)

    def seg(lo, hi):
        return _dot(hb, w_ref[:, lo:hi]) + b_ref[:, lo:hi]

    u = seg(0, D_MODEL) * _sigmoid(seg(D_MODEL, Q_COL))
    q = seg(Q_COL, K_COL)
    q_ref[...] = (q * (DK ** -0.5)).astype(BF16)
    if k_transposed:
        k = _dot_nt(wk_ref[...], hb) + bk_ref[:, 0:1]
    else:
        k = seg(K_COL, V_COL)
    k_ref[...] = k.astype(BF16)
    v = seg(V_COL, O_COL)
    v_ref[...] = v.astype(BF16)
    o = seg(O_COL, HEAD_COLS)
    so_ref[...] = _sigmoid(o).astype(BF16)
    if anchors is not None:
        for res, transposed in ((q, False), (k, k_transposed), (v, False), (o, False)):
            span = res.shape[0] if transposed else res.shape[1]
            for i in range(ANCHORS_PER_SEGMENT):
                at = i * (span // ANCHORS_PER_SEGMENT)
                anchors.append(res[at:at + SUBLANES, 0:LANES] if transposed else res[0:SUBLANES, at:at + LANES])
    t = seg(HEAD_COLS, HEAD_COLS + TAIL_W)
    gi_ref[...] = t[:, :LANES]
    gf_ref[...] = _log_sigmoid(pltpu.roll(t[:, :LANES], LANES - M_HEADS, 1))
    t = pltpu.roll(t, TAIL_W - 2 * M_HEADS, 1)
    gc_ref[...] = _sigmoid(t[:, :D_MODEL]).astype(BF16)
    gm_ref[...] = _sigmoid(t[:, D_MODEL:2 * D_MODEL]).astype(BF16)
    return u


def _zero_of(x):
    bits = pltpu.bitcast(x, jnp.uint32)
    bits = lax.shift_right_logical(lax.shift_right_logical(bits, jnp.uint32(16)), jnp.uint32(16))
    return pltpu.bitcast(bits, F32)[0:1, 0:1]


def _store_tokens(buf, first_token, u):
    for c in range(SUBLANES):
        buf[pl.ds(first_token * SUBLANES + c, u.shape[0], stride=SUBLANES), :] = u[:, c * LANES:(c + 1) * LANES]


def _inproj_body(x_ref, g_ref, w_ref, b_ref, u_ref, *out_refs):
    _store_tokens(u_ref, 0, _project(False, x_ref, g_ref, w_ref, b_ref, None, None, *out_refs))


_PIECES = ((D_QK, BF16), (D_QK, BF16), (D_M, BF16), (D_M, BF16), (D_MODEL, BF16), (D_MODEL, BF16),
           (LANES, F32), (LANES, F32))


def _inproj(x, g, w, b, layer, tm):
    n = x.shape[0]
    return pl.pallas_call(
        _inproj_body,
        grid=(n // tm,),
        in_specs=[_rows(tm, D_MODEL), _resident((1, D_MODEL)), _layer_of(w, layer), _layer_of(b, layer)],
        out_specs=[_rows(tm * SUBLANES, LANES)] + [_rows(tm, wd) for wd, _ in _PIECES],
        out_shape=[jax.ShapeDtypeStruct((n * SUBLANES, LANES), F32)]
        + [jax.ShapeDtypeStruct((n, wd), dt) for wd, dt in _PIECES],
        compiler_params=_cparams(1),
        name="inproj",
    )(x, g, w, b)


def _ln_silu(c, g, b):
    mu = jnp.mean(c, axis=-1, keepdims=True)
    xc = c - mu
    y = xc * lax.rsqrt(jnp.mean(xc * xc, axis=-1, keepdims=True) + EPS) * g + b
    return y * _sigmoid(y)


def _conv_taps(win, w_ref, b_ref, n_out):
    even = [b_ref[...]] * n_out
    odd = [None] * n_out
    for j in range(CONV_WIDTH):
        wj = w_ref[j * SUBLANES:(j + 1) * SUBLANES, :]
        prods = [win[t + j] * wj for t in range(n_out)]
        if j % 2 == 0:
            even = [a + p for a, p in zip(even, prods)]
        else:
            odd = [p if a is None else a + p for a, p in zip(odd, prods)]
    return [a + b for a, b in zip(even, odd)]


def _token_rows(buf, first_token, n):
    return jnp.concatenate(
        [buf[pl.ds(first_token * SUBLANES + c, n, stride=SUBLANES), :] for c in range(SUBLANES)], axis=1)


def _inproj_conv_body(x_ref, g_ref, w_ref, b_ref, wk_ref, bk_ref, wdw_ref, bdw_ref, gln_ref, bln_ref,
                      c_ref, hist_ref, *rest):
    piece_refs, (ubuf, cbuf) = rest[:-2], rest[-2:]
    tm = x_ref.shape[0]
    halo = HALO * SUBLANES

    @pl.when(pl.program_id(1) == 0)
    def _():
        ubuf[0:halo, :] = jnp.zeros((halo, LANES), F32)

    anchors = []
    u = _project(True, x_ref, g_ref, w_ref, b_ref, wk_ref, bk_ref, *piece_refs, anchors=anchors)
    part = tm // len(anchors)
    for i, a in enumerate(anchors):
        _store_tokens(ubuf, HALO + i * part, u[i * part:(i + 1) * part, :] + _zero_of(a))
    for t0 in range(0, tm, CONV_TOKENS):
        win = [ubuf[(t0 + HALO - CONV_BUF + k) * SUBLANES:(t0 + HALO - CONV_BUF + k + 1) * SUBLANES, :]
               for k in range(CONV_TOKENS + CONV_BUF)]
        for t, acc in enumerate(_conv_taps(win, wdw_ref, bdw_ref, CONV_TOKENS)):
            cbuf[(t0 + t) * SUBLANES:(t0 + t + 1) * SUBLANES, :] = acc
    for r0 in range(0, tm, CONV_ROWS):
        c_ref[r0:r0 + CONV_ROWS, :] = _ln_silu(
            _token_rows(cbuf, r0, CONV_ROWS), gln_ref[...], bln_ref[...]).astype(c_ref.dtype)

    @pl.when(pl.program_id(1) == pl.num_programs(1) - 1)
    def _():
        hist_ref[0] = _token_rows(ubuf, HALO + tm - CONV_BUF, CONV_BUF)

    ubuf[0:halo, :] = ubuf[tm * SUBLANES:tm * SUBLANES + halo, :]


def _inproj_conv(x, g, w, b, layer, wk_t, bk_t, wdw, bdw, gln, bln, batch, seq, tm):
    nt = seq // tm
    n = batch * seq

    def tile(width):
        return pl.BlockSpec((tm, width), lambda s, i: (s * nt + i, 0))

    pieces = list(_PIECES)
    piece_specs = [tile(wd) for wd, _ in pieces]
    piece_shapes = [jax.ShapeDtypeStruct((n, wd), dt) for wd, dt in pieces]
    piece_specs[1] = pl.BlockSpec((D_QK, tm), lambda s, i: (0, s * nt + i))
    piece_shapes[1] = jax.ShapeDtypeStruct((D_QK, n), BF16)
    return pl.pallas_call(
        _inproj_conv_body,
        grid=(batch, nt),
        in_specs=[tile(D_MODEL), _resident((1, D_MODEL)), _layer_of(w, layer), _layer_of(b, layer),
                  _resident(wk_t.shape), _resident(bk_t.shape), _resident(wdw.shape), _resident(bdw.shape),
                  _resident((1, D_MODEL)), _resident((1, D_MODEL))],
        out_specs=[tile(D_MODEL), pl.BlockSpec((1, CONV_BUF, D_MODEL), lambda s, i: (s, 0, 0))] + piece_specs,
        out_shape=[jax.ShapeDtypeStruct((n, D_MODEL), BF16),
                   jax.ShapeDtypeStruct((batch, CONV_BUF, D_MODEL), F32)] + piece_shapes,
        scratch_shapes=[pltpu.VMEM(((HALO + tm) * SUBLANES, LANES), F32),
                        pltpu.VMEM((tm * SUBLANES, LANES), F32)],
        compiler_params=_cparams(2),
        name="inproj_conv",
    )(x, g, w, b, wk_t, bk_t, wdw, bdw, gln, bln)


def _conv_sample_body(steps, st_ref, u_ref, w_ref, b_ref, gln_ref, bln_ref, c_ref, hist_ref, fbuf, cbuf):
    nseq = st_ref.shape[0]
    full = CONV_BUF + steps
    for s in range(nseq):
        f0 = s * full
        for c in range(SUBLANES):
            fbuf[pl.ds(f0 * SUBLANES + c, CONV_BUF, stride=SUBLANES), :] = st_ref[s, :, c * LANES:(c + 1) * LANES]
        fbuf[(f0 + CONV_BUF) * SUBLANES:(f0 + full) * SUBLANES, :] = (
            u_ref[s * steps * SUBLANES:(s + 1) * steps * SUBLANES, :])
        win = [fbuf[(f0 + k) * SUBLANES:(f0 + k + 1) * SUBLANES, :] for k in range(full)]
        for t, acc in enumerate(_conv_taps(win, w_ref, b_ref, steps)):
            cbuf[(s * steps + t) * SUBLANES:(s * steps + t + 1) * SUBLANES, :] = acc
        hist_ref[s] = _token_rows(fbuf, f0 + steps, CONV_BUF)
    c_ref[...] = _ln_silu(_token_rows(cbuf, 0, nseq * steps), gln_ref[...], bln_ref[...])


def _drop_refs(body, first, count, *refs):
    return body(*refs[:first], *refs[first + count:])


def _stacked_call(body, args, in_specs, stacked_outs, prev, **kwargs):
    aliases = {}
    if prev is not None:
        first = len(args)
        in_specs = list(in_specs) + [pl.BlockSpec(memory_space=pl.ANY)] * len(prev)
        aliases = {first + j: out for j, out in enumerate(stacked_outs)}
        args = list(args) + list(prev)
        body = functools.partial(_drop_refs, body, first, len(prev))
    return pl.pallas_call(body, in_specs=in_specs, input_output_aliases=aliases, **kwargs)(*args)


def _conv_sample(state_all, u, wdw, bdw, gln, bln, layer, prev, nseq):
    batch = state_all.shape[1]
    steps = u.shape[0] // (batch * SUBLANES)
    hist_spec = pl.BlockSpec((None, nseq, CONV_BUF, D_MODEL), lambda i: (layer, i, 0, 0))
    return _stacked_call(
        functools.partial(_conv_sample_body, steps),
        [state_all, u, wdw, bdw, gln, bln],
        [hist_spec, _rows(nseq * steps * SUBLANES, LANES), _resident(wdw.shape), _resident(bdw.shape),
         _resident((1, D_MODEL)), _resident((1, D_MODEL))],
        [1], prev,
        grid=(batch // nseq,),
        out_specs=[_rows(nseq * steps, D_MODEL), hist_spec],
        out_shape=[jax.ShapeDtypeStruct((batch * steps, D_MODEL), F32),
                   jax.ShapeDtypeStruct(state_all.shape, F32)],
        scratch_shapes=[pltpu.VMEM((nseq * (CONV_BUF + steps) * SUBLANES, LANES), F32),
                        pltpu.VMEM((nseq * steps * SUBLANES, LANES), F32)],
        compiler_params=_cparams(1),
        name="conv_sample",
    )


def _col(x, h):
    return x[:, h:h + 1]


def _with_ones(vh):
    lane = lax.broadcasted_iota(jnp.int32, (vh.shape[0], LANES), 1)
    return jnp.concatenate([vh, (lane == 0).astype(vh.dtype)], axis=1)


def _intra_head(s, vaug, b_col, r_row, inter_col, mask):
    dm = jnp.where(mask, b_col + r_row, NEG)
    m_t = jnp.maximum(inter_col, jnp.max(dm, axis=1, keepdims=True))
    sw = s * jnp.exp(dm - m_t)
    return _dot(sw.astype(BF16), vaug), m_t, jnp.exp(inter_col - m_t)


def _head_out(num_aug, m_t, gmh, so):
    num = num_aug[:, :DV]
    den = num_aug[:, DV:DV + 1]
    inv = 1.0 / jnp.maximum(jnp.abs(den), jnp.exp(-m_t))
    scale = inv * lax.rsqrt(inv * inv * jnp.mean(num * num, axis=-1, keepdims=True) + EPS)
    return (num * scale * (gmh * so.astype(F32))).astype(BF16)


def _mlstm_prompt_body(q_ref, kt_ref, v_ref, so_ref, gi_ref, gf_ref, gmh_ref,
                       hmo_ref, c_ref, n_ref, m_ref, ct_ref):
    L = CHUNK
    step = pl.program_id(1)

    @pl.when(step == 0)
    def _():
        ct_ref[...] = jnp.zeros(ct_ref.shape, F32)
        m_ref[...] = jnp.zeros(m_ref.shape, F32)

    row = lax.broadcasted_iota(jnp.int32, (L, LANES), 0)
    tri = lax.broadcasted_iota(jnp.int32, (L, L), 1) <= lax.broadcasted_iota(jnp.int32, (L, L), 0)
    heads = range(M_HEADS)
    m_prev = m_ref[0]
    for t0 in range(0, q_ref.shape[0], L):
        gi = gi_ref[t0:t0 + L, :]
        b = gf_ref[t0:t0 + L, :]
        sh = 1
        while sh < L:
            b = b + jnp.where(row >= sh, pltpu.roll(b, sh, 0), 0.0)
            sh *= 2
        r = gi - b
        r_t = r.T
        b_last = b[L - 1:L, :]
        m_new = jnp.maximum(b_last + m_prev, jnp.max(b_last + r, axis=0, keepdims=True))
        dec = jnp.exp(b_last + m_prev - m_new)
        inter = b + m_prev
        qs = [q_ref[t0:t0 + L, h * DK:(h + 1) * DK] for h in heads]
        kts = [kt_ref[h * DK:(h + 1) * DK, t0:t0 + L] for h in heads]
        vaugs = [_with_ones(v_ref[t0:t0 + L, h * DV:(h + 1) * DV]) for h in heads]
        scores = [_dot(qs[h], kts[h]) for h in heads]
        ct_olds = [ct_ref[h] for h in heads]
        reads = [_dot(qs[h], ct_olds[h].astype(BF16)) for h in heads]
        for h in heads:
            wk_row = jnp.exp(_col(b_last, h) + r_t[h:h + 1, :] - _col(m_new, h))
            ct_ref[h] = _col(dec, h) * ct_olds[h] + _dot((kts[h].astype(F32) * wk_row).astype(BF16), vaugs[h])
        for h in heads:
            num_aug, m_t, w_int = _intra_head(
                scores[h], vaugs[h], _col(b, h), r_t[h:h + 1, :], _col(inter, h), tri)
            hmo_ref[t0:t0 + L, h * DV:(h + 1) * DV] = _head_out(
                num_aug + w_int * reads[h], m_t, gmh_ref[:, h * DV:(h + 1) * DV],
                so_ref[t0:t0 + L, h * DV:(h + 1) * DV])
        m_prev = m_new
    m_ref[0] = m_prev

    @pl.when(step == pl.num_programs(1) - 1)
    def _():
        for h in range(M_HEADS):
            ct = ct_ref[h]
            c_ref[0, h] = ct[:, :DV].T
            n_ref[0, h:h + 1, :] = ct[:, DV:].T[0:1, :]


def _mlstm_prompt(q, kt, v, so, gi, gf, gmh, batch, seq, depth, layer, prev):
    rows = MLSTM_CHUNKS * CHUNK
    nc = seq // rows
    n = batch * seq

    def tok(width):
        return pl.BlockSpec((rows, width), lambda b, c: (b * nc + c, 0))

    return _stacked_call(
        _mlstm_prompt_body,
        [q, kt, v, so, gi, gf, gmh],
        [tok(D_QK), pl.BlockSpec((D_QK, rows), lambda b, c: (0, b * nc + c)),
         tok(D_M), tok(D_M), tok(LANES), tok(LANES), _resident((1, D_M))],
        [1, 2], prev,
        grid=(batch, nc),
        out_specs=[tok(D_M),
                   pl.BlockSpec((None, 1, M_HEADS, DV, DK), lambda b, c: (layer, b, 0, 0, 0)),
                   pl.BlockSpec((None, 1, M_HEADS, DK), lambda b, c: (layer, b, 0, 0)),
                   pl.BlockSpec((1, 1, LANES), lambda b, c: (b, 0, 0))],
        out_shape=[jax.ShapeDtypeStruct((n, D_M), BF16),
                   jax.ShapeDtypeStruct((depth, batch, M_HEADS, DV, DK), F32),
                   jax.ShapeDtypeStruct((depth, batch, M_HEADS, DK), F32),
                   jax.ShapeDtypeStruct((batch, 1, LANES), F32)],
        scratch_shapes=[pltpu.VMEM((M_HEADS, DK, AUG), F32)],
        compiler_params=_cparams(2),
        name="mlstm_prompt",
    )


def _mlstm_sample_intra_body(steps, q_ref, k_ref, v_ref, gi_ref, gf_ref, mtok_ref,
                             num_ref, mt_ref, wi_ref, wk_ref, wkt_ref, dec_ref, mnew_ref):
    L = q_ref.shape[0]
    row = lax.broadcasted_iota(jnp.int32, (L, LANES), 0)
    pos = row & (steps - 1)

    def down(x, k):
        return pltpu.roll(x, k, 0)

    def up(x, k):
        return pltpu.roll(x, L - k, 0)

    gi = gi_ref[...]
    b = gf_ref[...]
    sh = 1
    while sh < steps:
        b = b + jnp.where(pos >= sh, down(b, sh), 0.0)
        sh *= 2
    r = gi - b
    r_t = r.T
    m_prev = mtok_ref[...]
    b_last = b
    for k in range(1, steps):
        b_last = jnp.where(pos == steps - 1 - k, up(b, k), b_last)
    d_last = b_last + r
    d_max = d_last
    sh = 1
    while sh < steps:
        d_max = jnp.maximum(d_max, jnp.where((pos & sh) == 0, up(d_max, sh), down(d_max, sh)))
        sh *= 2
    m_new = jnp.maximum(b_last + m_prev, d_max)
    wk = jnp.exp(d_last - m_new)
    inter = b + m_prev
    ti = lax.broadcasted_iota(jnp.int32, (L, L), 0)
    si = lax.broadcasted_iota(jnp.int32, (L, L), 1)
    mask = (si <= ti) & ((si & -steps) == (ti & -steps))
    lane = lax.broadcasted_iota(jnp.int32, (L, LANES), 1)

    mt_all = jnp.zeros((L, LANES), F32)
    wi_all = jnp.zeros((L, LANES), F32)
    for h in range(M_HEADS):
        s = _dot_nt(q_ref[:, h * DK:(h + 1) * DK], k_ref[:, h * DK:(h + 1) * DK])
        num_aug, m_t, w_int = _intra_head(s, _with_ones(v_ref[:, h * DV:(h + 1) * DV]),
                                          _col(b, h), r_t[h:h + 1, :], _col(inter, h), mask)
        num_ref[:, h * AUG:(h + 1) * AUG] = num_aug
        mt_all = jnp.where(lane == h, m_t, mt_all)
        wi_all = jnp.where(lane == h, w_int, wi_all)
    mt_ref[...] = mt_all
    wi_ref[...] = wi_all
    wk_ref[...] = wk
    wkt_ref[...] = wk.T[0:8, :]
    dec_ref[...] = jnp.exp(b_last + m_prev - m_new)
    mnew_ref[...] = m_new


def _mlstm_sample_intra(q, k, v, gi, gf, mtok, steps):
    n = q.shape[0]
    L = CHUNK
    scal = jax.ShapeDtypeStruct((n, LANES), F32)
    return pl.pallas_call(
        functools.partial(_mlstm_sample_intra_body, steps),
        grid=(n // L,),
        in_specs=[_rows(L, D_QK), _rows(L, D_QK), _rows(L, D_M), _rows(L, LANES), _rows(L, LANES),
                  _rows(L, LANES)],
        out_specs=[_rows(L, M_HEADS * AUG), _rows(L, LANES), _rows(L, LANES), _rows(L, LANES),
                   pl.BlockSpec((8, L), lambda i: (0, i)), _rows(L, LANES), _rows(L, LANES)],
        out_shape=[jax.ShapeDtypeStruct((n, M_HEADS * AUG), F32), scal, scal, scal,
                   jax.ShapeDtypeStruct((8, n), F32), scal, scal],
        compiler_params=_cparams(1),
        name="mlstm_sample_intra",
    )(q, k, v, gi, gf, mtok)


def _mlstm_sample_state_body(steps, q_ref, kwin_ref, vt_ref, so_ref, num_ref, mt_ref, wi_ref,
                             wk_ref, wkt_ref, dec_ref, gmh_ref, c_ref, n_ref,
                             hmo_ref, cn_ref, nn_ref):
    rows = q_ref.shape[0]
    win = kwin_ref.shape[0]
    nseq = rows // steps
    first = (pl.program_id(0) % (win // rows)) * rows
    grp = lax.broadcasted_iota(jnp.int32, (rows, 1), 0) & -steps
    lane = lax.broadcasted_iota(jnp.int32, (1, win), 1)
    col = lax.broadcasted_iota(jnp.int32, (rows, LANES), 1)
    wi = wi_ref[...]
    for h in range(M_HEADS):
        qh = q_ref[:, h * DK:(h + 1) * DK]
        qf = qh.astype(F32)
        kwin = kwin_ref[:, h * DK:(h + 1) * DK]
        k_own = kwin_ref[pl.ds(pl.multiple_of(first, rows), rows), h * DK:(h + 1) * DK].astype(F32)
        vt = vt_ref[h * DV:(h + 1) * DV, :].astype(F32)
        wk_col = _col(wk_ref[...], h)
        wk_row = wkt_ref[h:h + 1, :]
        cq = jnp.zeros((rows, DV), F32)
        nq = jnp.zeros((rows, 1), F32)
        for s in range(nseq):
            c_old = c_ref[s, h]
            n_old = n_ref[s, h:h + 1, :]
            own = grp == s * steps
            cq = jnp.where(own, _dot_nt(qh, c_old.astype(BF16)), cq)
            nq = jnp.where(own, jnp.sum(qf * n_old, axis=1, keepdims=True), nq)
            dec = dec_ref[s * steps:s * steps + 1, h:h + 1]
            lo = first + s * steps
            wrow = jnp.where((lane >= lo) & (lane < lo + steps), wk_row, 0.0)
            cn_ref[s, h] = dec * c_old + _dot((vt * wrow).astype(BF16), kwin)
            nn_ref[s, h:h + 1, :] = dec * n_old + jnp.sum(
                k_own * jnp.where(own, wk_col, 0.0), axis=0, keepdims=True)
        inter_aug = jnp.concatenate([cq, jnp.where(col == 0, nq, 0.0)], axis=1)
        num_aug = num_ref[:, h * AUG:(h + 1) * AUG] + _col(wi, h) * inter_aug
        hmo_ref[:, h * DV:(h + 1) * DV] = _head_out(
            num_aug, _col(mt_ref[...], h), gmh_ref[:, h * DV:(h + 1) * DV], so_ref[:, h * DV:(h + 1) * DV])


def _mlstm_sample_state(q, k, vt, so, num, mt, wi, wk, wkt, dec, gmh, c_all, n_all, layer, prev, steps):
    n = q.shape[0]
    batch = c_all.shape[1]
    rows = SEQ_PER_STEP * steps
    per_win = CHUNK // rows

    def win_rows(width):
        return pl.BlockSpec((CHUNK, width), lambda i: (i // per_win, 0))

    c_spec = pl.BlockSpec((None, SEQ_PER_STEP, M_HEADS, DV, DK), lambda i: (layer, i, 0, 0, 0))
    n_spec = pl.BlockSpec((None, SEQ_PER_STEP, M_HEADS, DK), lambda i: (layer, i, 0, 0))
    return _stacked_call(
        functools.partial(_mlstm_sample_state_body, steps),
        [q, k, vt, so, num, mt, wi, wk, wkt, dec, gmh, c_all, n_all],
        [_rows(rows, D_QK), win_rows(D_QK),
         pl.BlockSpec((D_M, CHUNK), lambda i: (0, i // per_win)),
         _rows(rows, D_M), _rows(rows, M_HEADS * AUG), _rows(rows, LANES),
         _rows(rows, LANES), _rows(rows, LANES),
         pl.BlockSpec((8, CHUNK), lambda i: (0, i // per_win)),
         _rows(rows, LANES), _resident((1, D_M)), c_spec, n_spec],
        [1, 2], prev,
        grid=(batch // SEQ_PER_STEP,),
        out_specs=[_rows(rows, D_M), c_spec, n_spec],
        out_shape=[jax.ShapeDtypeStruct((n, D_M), BF16),
                   jax.ShapeDtypeStruct(c_all.shape, F32),
                   jax.ShapeDtypeStruct(n_all.shape, F32)],
        compiler_params=_cparams(1),
        name="mlstm_sample_state",
    )


def _merge_body(x_ref, c_ref, hmo_ref, gc_ref, gm_ref, wco_ref, wmo_ref, wo_ref, gpm_ref, y_ref):
    y_conv = _dot(c_ref[...].astype(BF16), wco_ref[...])
    y_m = _dot(hmo_ref[...], wmo_ref[...])
    mix = gc_ref[...].astype(F32) * y_conv + gm_ref[...].astype(F32) * y_m
    y_ref[...] = x_ref[...] + _rmsnorm(_dot(mix.astype(BF16), wo_ref[...]), gpm_ref[...])


def _merge(x, c, hmo, gc, gm, wco, wmo, wo, layer, gpm, tm):
    n = x.shape[0]
    return pl.pallas_call(
        _merge_body,
        grid=(n // tm,),
        in_specs=[_rows(tm, D_MODEL), _rows(tm, D_MODEL), _rows(tm, D_M), _rows(tm, D_MODEL),
                  _rows(tm, D_MODEL),
                  _layer_of(wco, layer), _layer_of(wmo, layer), _layer_of(wo, layer),
                  _resident((1, D_MODEL))],
        out_specs=_rows(tm, D_MODEL),
        out_shape=jax.ShapeDtypeStruct((n, D_MODEL), F32),
        compiler_params=_cparams(1),
        name="merge",
    )(x, c, hmo, gc, gm, wco, wmo, wo, gpm)


def _ffn_body(x_ref, gpf_ref, w1_ref, b1_ref, w2_ref, b2_ref, gpo_ref, y_ref):
    x1 = x_ref[...]
    hf = _rmsnorm(x1, gpf_ref[...]).astype(BF16)
    a = jnp.maximum(_dot(hf, w1_ref[...]) + b1_ref[...], 0.0)
    ff = _dot((a * a).astype(BF16), w2_ref[...]) + b2_ref[...]
    y_ref[...] = x1 + _rmsnorm(ff, gpo_ref[...])


def _ffn(x, gpf, w1, b1, w2, b2, layer, gpo, tm):
    n = x.shape[0]
    return pl.pallas_call(
        _ffn_body,
        grid=(n // tm,),
        in_specs=[_rows(tm, D_MODEL), _resident((1, D_MODEL)),
                  _layer_of(w1, layer), _resident((1, D_FF)),
                  _layer_of(w2, layer), _resident((1, D_MODEL)), _resident((1, D_MODEL))],
        out_specs=_rows(tm, D_MODEL),
        out_shape=jax.ShapeDtypeStruct((n, D_MODEL), F32),
        compiler_params=_cparams(1),
        name="ffn",
    )(x, gpf, w1, b1, w2, b2, gpo)


def _row(v):
    return v.reshape(1, -1)


def _prep_weights(w_in, b_in, w_conv_out, w_m_out, w_o, w_ff1, w_ff2):
    pad = HEAD_COLS + TAIL_W - w_in.shape[2]
    return dict(w_main=jnp.pad(w_in.astype(BF16), ((0, 0), (0, 0), (0, pad))),
                b_main=jnp.pad(b_in, ((0, 0), (0, pad)))[:, None, :],
                wco=w_conv_out.astype(BF16), wmo=w_m_out.astype(BF16), wo=w_o.astype(BF16),
                w1=w_ff1.astype(BF16), w2=w_ff2.astype(BF16))


def kernel(x_prompt, x_sample, state_conv, state_C, state_n, state_m, g_pre_mix, w_in, b_in, w_dw, b_dw,
           g_cln, b_cln, w_conv_out, g_mh, w_m_out, w_o, g_post_mix, g_pre_ff, w_ff1, b_ff1, w_ff2, b_ff2,
           g_post_ff):
    bp, tp, _ = x_prompt.shape
    bs, ts, _ = x_sample.shape
    depth = w_in.shape[0]
    assert tp % (MLSTM_CHUNKS * CHUNK) == 0 and tp >= CONV_BUF and CHUNK % (SEQ_PER_STEP * ts) == 0
    assert (bs * ts) % CHUNK == 0 and bs % SEQ_PER_STEP == 0 and ts & (ts - 1) == 0
    tm_p = min(256, bp * tp)
    tm_s = min(256, bs * ts)
    assert (bp * tp) % tm_p == 0 and (bs * ts) % tm_s == 0
    yp = x_prompt.reshape(bp * tp, D_MODEL)
    ys = x_sample.reshape(bs * ts, D_MODEL)
    p = _prep_weights(w_in, b_in, w_conv_out, w_m_out, w_o, w_ff1, w_ff2)
    hist_p, m_p, m_s = [], [], []
    prompt_state = sample_state = sample_hist = None
    for d in range(depth):
        conv_args = (w_dw[d].reshape(CONV_WIDTH * SUBLANES, LANES), b_dw[d].reshape(SUBLANES, LANES),
                     _row(g_cln[d]), _row(b_cln[d]))
        gmh = _row(g_mh[d])
        inproj_args = (_row(g_pre_mix[d]), p["w_main"], p["b_main"], d)
        wk_t = w_in[d, :, K_COL:V_COL].T.astype(BF16)
        bk_t = jnp.broadcast_to(b_in[d, K_COL:V_COL].reshape(D_QK, 1), (D_QK, LANES))

        def tail(x, c, hmo, gc, gm, tm):
            x1 = _merge(x, c, hmo, gc, gm, p["wco"], p["wmo"], p["wo"], d, _row(g_post_mix[d]), tm)
            return _ffn(x1, _row(g_pre_ff[d]), p["w1"], _row(b_ff1[d]), p["w2"], _row(b_ff2[d]), d,
                        _row(g_post_ff[d]), tm)

        c, hist, q, kt, v, so, gc, gm, gi, gf = _inproj_conv(
            yp, *inproj_args, wk_t, bk_t, *conv_args, bp, tp, tm_p)
        hmo, *prompt_state, m_new = _mlstm_prompt(q, kt, v, so, gi, gf, gmh, bp, tp, depth, d, prompt_state)
        yp = tail(yp, c, hmo, gc, gm, 2 * tm_p)
        hist_p.append(hist)
        m_p.append(m_new[:, 0, :M_HEADS])

        u, q, k, v, so, gc, gm, gi, gf = _inproj(ys, *inproj_args, tm_s)
        c, *sample_hist = _conv_sample(state_conv, u, *conv_args, d, sample_hist, min(16, bs))
        mtok = jnp.pad(jnp.repeat(state_m[d], ts, axis=0), ((0, 0), (0, LANES - M_HEADS)))
        num, mt, wi, wk, wkt, dec, mnew = _mlstm_sample_intra(q, k, v, gi, gf, mtok, ts)
        hmo, *sample_state = _mlstm_sample_state(q, k, v.T, so, num, mt, wi, wk, wkt, dec, gmh,
                                                 state_C, state_n, d, sample_state, ts)
        ys = tail(ys, c, hmo, gc, gm, tm_s)
        m_s.append(mnew[ts - 1::ts, :M_HEADS])
    return (yp.reshape(bp, tp, D_MODEL), ys.reshape(bs, ts, D_MODEL),
            jnp.stack(hist_p), prompt_state[0], prompt_state[1], jnp.stack(m_p),
            sample_hist[0], sample_state[0], sample_state[1], jnp.stack(m_s))
```

```python
import functools

import jax
import jax.numpy as jnp
from jax import lax
from jax.experimental import pallas as pl
from jax.experimental.pallas import tpu as pltpu

F32 = jnp.float32
BF16 = jnp.bfloat16

D_MODEL = 1024
M_HEADS = 4
DK = 256
DV = 512
D_QK = M_HEADS * DK
D_M = M_HEADS * DV
D_FF = 4 * D_MODEL
CONV_WIDTH = 31
CONV_BUF = CONV_WIDTH - 1
CHUNK = 128
EPS = 1e-6
NEG = -1e30
LANES = 128
SUBLANES = 8
Q_COL = 2 * D_MODEL
K_COL = Q_COL + D_QK
V_COL = K_COL + D_QK
O_COL = V_COL + D_M
HEAD_COLS = O_COL + D_M
TAIL_W = -(-(2 * M_HEADS + 2 * D_MODEL) // LANES) * LANES
AUG = DV + LANES
HALO = 32
CONV_TOKENS = 8
CONV_ROWS = 16
ANCHORS_PER_SEGMENT = 4
SEQ_PER_STEP = 4
MLSTM_CHUNKS = 2

VMEM_LIMIT = 56 * 1024 * 1024


def _cparams(n_axes):
    return pltpu.CompilerParams(dimension_semantics=("arbitrary",) * n_axes,
                                vmem_limit_bytes=VMEM_LIMIT)


def _resident(shape):
    nd = len(shape)
    return pl.BlockSpec(shape, lambda *_: (0,) * nd, pipeline_mode=pl.Buffered(1))


def _layer_of(w, layer):
    tail = (0,) * (w.ndim - 1)
    return pl.BlockSpec((None,) + w.shape[1:], lambda *_: (layer,) + tail, pipeline_mode=pl.Buffered(1))


def _rows(tm, width):
    return pl.BlockSpec((tm, width), lambda i: (i, 0))


def _rmsnorm(x, g):
    return x * lax.rsqrt(jnp.mean(x * x, axis=-1, keepdims=True) + EPS) * g


def _sigmoid(x):
    return 1.0 / (1.0 + jnp.exp(-x))


def _log_sigmoid(x):
    return jnp.minimum(x, 0.0) - jnp.log(1.0 + jnp.exp(-jnp.abs(x)))


def _dot(a, b):
    return jnp.dot(a, b, preferred_element_type=F32)


def _dot_nt(a, b):
    return lax.dot_general(a, b, (((1,), (1,)), ((), ())), preferred_element_type=F32)


def _project(k_transposed, x_ref, g_ref, w_ref, b_ref, wk_ref, bk_ref,
             q_ref, k_ref, v_ref, so_ref, gc_ref, gm_ref, gi_ref, gf_ref, anchors=None):
    hb = _rmsnorm(x_ref[...], g_ref[...]).astype(BF16)

    def seg(lo, hi):
        return _dot(hb, w_ref[:, lo:hi]) + b_ref[:, lo:hi]

    u = seg(0, D_MODEL) * _sigmoid(seg(D_MODEL, Q_COL))
    q = seg(Q_COL, K_COL)
    q_ref[...] = (q * (DK ** -0.5)).astype(BF16)
    if k_transposed:
        k = _dot_nt(wk_ref[...], hb) + bk_ref[:, 0:1]
    else:
        k = seg(K_COL, V_COL)
    k_ref[...] = k.astype(BF16)
    v = seg(V_COL, O_COL)
    v_ref[...] = v.astype(BF16)
    o = seg(O_COL, HEAD_COLS)
    so_ref[...] = _sigmoid(o).astype(BF16)
    if anchors is not None:
        for res, transposed in ((q, False), (k, k_transposed), (v, False), (o, False)):
            span = res.shape[0] if transposed else res.shape[1]
            for i in range(ANCHORS_PER_SEGMENT):
                at = i * (span // ANCHORS_PER_SEGMENT)
                anchors.append(res[at:at + SUBLANES, 0:LANES] if transposed else res[0:SUBLANES, at:at + LANES])
    t = seg(HEAD_COLS, HEAD_COLS + TAIL_W)
    gi_ref[...] = t[:, :LANES]
    gf_ref[...] = _log_sigmoid(pltpu.roll(t[:, :LANES], LANES - M_HEADS, 1))
    t = pltpu.roll(t, TAIL_W - 2 * M_HEADS, 1)
    gc_ref[...] = _sigmoid(t[:, :D_MODEL]).astype(BF16)
    gm_ref[...] = _sigmoid(t[:, D_MODEL:2 * D_MODEL]).astype(BF16)
    return u


def _zero_of(x):
    bits = pltpu.bitcast(x, jnp.uint32)
    bits = lax.shift_right_logical(lax.shift_right_logical(bits, jnp.uint32(16)), jnp.uint32(16))
    return pltpu.bitcast(bits, F32)[0:1, 0:1]


def _store_tokens(buf, first_token, u):
    for c in range(SUBLANES):
        buf[pl.ds(first_token * SUBLANES + c, u.shape[0], stride=SUBLANES), :] = u[:, c * LANES:(c + 1) * LANES]


def _inproj_body(x_ref, g_ref, w_ref, b_ref, u_ref, *out_refs):
    _store_tokens(u_ref, 0, _project(False, x_ref, g_ref, w_ref, b_ref, None, None, *out_refs))


_PIECES = ((D_QK, BF16), (D_QK, BF16), (D_M, BF16), (D_M, BF16), (D_MODEL, BF16), (D_MODEL, BF16),
           (LANES, F32), (LANES, F32))


def _inproj(x, g, w, b, layer, tm):
    n = x.shape[0]
    return pl.pallas_call(
        _inproj_body,
        grid=(n // tm,),
        in_specs=[_rows(tm, D_MODEL), _resident((1, D_MODEL)), _layer_of(w, layer), _layer_of(b, layer)],
        out_specs=[_rows(tm * SUBLANES, LANES)] + [_rows(tm, wd) for wd, _ in _PIECES],
        out_shape=[jax.ShapeDtypeStruct((n * SUBLANES, LANES), F32)]
        + [jax.ShapeDtypeStruct((n, wd), dt) for wd, dt in _PIECES],
        compiler_params=_cparams(1),
        name="inproj",
    )(x, g, w, b)


def _ln_silu(c, g, b):
    mu = jnp.mean(c, axis=-1, keepdims=True)
    xc = c - mu
    y = xc * lax.rsqrt(jnp.mean(xc * xc, axis=-1, keepdims=True) + EPS) * g + b
    return y * _sigmoid(y)


def _conv_taps(win, w_ref, b_ref, n_out):
    even = [b_ref[...]] * n_out
    odd = [None] * n_out
    for j in range(CONV_WIDTH):
        wj = w_ref[j * SUBLANES:(j + 1) * SUBLANES, :]
        prods = [win[t + j] * wj for t in range(n_out)]
        if j % 2 == 0:
            even = [a + p for a, p in zip(even, prods)]
        else:
            odd = [p if a is None else a + p for a, p in zip(odd, prods)]
    return [a + b for a, b in zip(even, odd)]


def _token_rows(buf, first_token, n):
    return jnp.concatenate(
        [buf[pl.ds(first_token * SUBLANES + c, n, stride=SUBLANES), :] for c in range(SUBLANES)], axis=1)


def _inproj_conv_body(x_ref, g_ref, w_ref, b_ref, wk_ref, bk_ref, wdw_ref, bdw_ref, gln_ref, bln_ref,
                      c_ref, hist_ref, *rest):
    piece_refs, (ubuf, cbuf) = rest[:-2], rest[-2:]
    tm = x_ref.shape[0]
    halo = HALO * SUBLANES

    @pl.when(pl.program_id(1) == 0)
    def _():
        ubuf[0:halo, :] = jnp.zeros((halo, LANES), F32)

    anchors = []
    u = _project(True, x_ref, g_ref, w_ref, b_ref, wk_ref, bk_ref, *piece_refs, anchors=anchors)
    part = tm // len(anchors)
    for i, a in enumerate(anchors):
        _store_tokens(ubuf, HALO + i * part, u[i * part:(i + 1) * part, :] + _zero_of(a))
    for t0 in range(0, tm, CONV_TOKENS):
        win = [ubuf[(t0 + HALO - CONV_BUF + k) * SUBLANES:(t0 + HALO - CONV_BUF + k + 1) * SUBLANES, :]
               for k in range(CONV_TOKENS + CONV_BUF)]
        for t, acc in enumerate(_conv_taps(win, wdw_ref, bdw_ref, CONV_TOKENS)):
            cbuf[(t0 + t) * SUBLANES:(t0 + t + 1) * SUBLANES, :] = acc
    for r0 in range(0, tm, CONV_ROWS):
        c_ref[r0:r0 + CONV_ROWS, :] = _ln_silu(
            _token_rows(cbuf, r0, CONV_ROWS), gln_ref[...], bln_ref[...]).astype(c_ref.dtype)

    @pl.when(pl.program_id(1) == pl.num_programs(1) - 1)
    def _():
        hist_ref[0] = _token_rows(ubuf, HALO + tm - CONV_BUF, CONV_BUF)

    ubuf[0:halo, :] = ubuf[tm * SUBLANES:tm * SUBLANES + halo, :]


def _inproj_conv(x, g, w, b, layer, wk_t, bk_t, wdw, bdw, gln, bln, batch, seq, tm):
    nt = seq // tm
    n = batch * seq

    def tile(width):
        return pl.BlockSpec((tm, width), lambda s, i: (s * nt + i, 0))

    pieces = list(_PIECES)
    piece_specs = [tile(wd) for wd, _ in pieces]
    piece_shapes = [jax.ShapeDtypeStruct((n, wd), dt) for wd, dt in pieces]
    piece_specs[1] = pl.BlockSpec((D_QK, tm), lambda s, i: (0, s * nt + i))
    piece_shapes[1] = jax.ShapeDtypeStruct((D_QK, n), BF16)
    return pl.pallas_call(
        _inproj_conv_body,
        grid=(batch, nt),
        in_specs=[tile(D_MODEL), _resident((1, D_MODEL)), _layer_of(w, layer), _layer_of(b, layer),
                  _resident(wk_t.shape), _resident(bk_t.shape), _resident(wdw.shape), _resident(bdw.shape),
                  _resident((1, D_MODEL)), _resident((1, D_MODEL))],
        out_specs=[tile(D_MODEL), pl.BlockSpec((1, CONV_BUF, D_MODEL), lambda s, i: (s, 0, 0))] + piece_specs,
        out_shape=[jax.ShapeDtypeStruct((n, D_MODEL), BF16),
                   jax.ShapeDtypeStruct((batch, CONV_BUF, D_MODEL), F32)] + piece_shapes,
        scratch_shapes=[pltpu.VMEM(((HALO + tm) * SUBLANES, LANES), F32),
                        pltpu.VMEM((tm * SUBLANES, LANES), F32)],
        compiler_params=_cparams(2),
        name="inproj_conv",
    )(x, g, w, b, wk_t, bk_t, wdw, bdw, gln, bln)


def _conv_sample_body(steps, st_ref, u_ref, w_ref, b_ref, gln_ref, bln_ref, c_ref, hist_ref, fbuf, cbuf):
    nseq = st_ref.shape[0]
    full = CONV_BUF + steps
    for s in range(nseq):
        f0 = s * full
        for c in range(SUBLANES):
            fbuf[pl.ds(f0 * SUBLANES + c, CONV_BUF, stride=SUBLANES), :] = st_ref[s, :, c * LANES:(c + 1) * LANES]
        fbuf[(f0 + CONV_BUF) * SUBLANES:(f0 + full) * SUBLANES, :] = (
            u_ref[s * steps * SUBLANES:(s + 1) * steps * SUBLANES, :])
        win = [fbuf[(f0 + k) * SUBLANES:(f0 + k + 1) * SUBLANES, :] for k in range(full)]
        for t, acc in enumerate(_conv_taps(win, w_ref, b_ref, steps)):
            cbuf[(s * steps + t) * SUBLANES:(s * steps + t + 1) * SUBLANES, :] = acc
        hist_ref[s] = _token_rows(fbuf, f0 + steps, CONV_BUF)
    c_ref[...] = _ln_silu(_token_rows(cbuf, 0, nseq * steps), gln_ref[...], bln_ref[...])


def _drop_refs(body, first, count, *refs):
    return body(*refs[:first], *refs[first + count:])


def _stacked_call(body, args, in_specs, stacked_outs, prev, **kwargs):
    aliases = {}
    if prev is not None:
        first = len(args)
        in_specs = list(in_specs) + [pl.BlockSpec(memory_space=pl.ANY)] * len(prev)
        aliases = {first + j: out for j, out in enumerate(stacked_outs)}
        args = list(args) + list(prev)
        body = functools.partial(_drop_refs, body, first, len(prev))
    return pl.pallas_call(body, in_specs=in_specs, input_output_aliases=aliases, **kwargs)(*args)


def _conv_sample(state_all, u, wdw, bdw, gln, bln, layer, prev, nseq):
    batch = state_all.shape[1]
    steps = u.shape[0] // (batch * SUBLANES)
    hist_spec = pl.BlockSpec((None, nseq, CONV_BUF, D_MODEL), lambda i: (layer, i, 0, 0))
    return _stacked_call(
        functools.partial(_conv_sample_body, steps),
        [state_all, u, wdw, bdw, gln, bln],
        [hist_spec, _rows(nseq * steps * SUBLANES, LANES), _resident(wdw.shape), _resident(bdw.shape),
         _resident((1, D_MODEL)), _resident((1, D_MODEL))],
        [1], prev,
        grid=(batch // nseq,),
        out_specs=[_rows(nseq * steps, D_MODEL), hist_spec],
        out_shape=[jax.ShapeDtypeStruct((batch * steps, D_MODEL), F32),
                   jax.ShapeDtypeStruct(state_all.shape, F32)],
        scratch_shapes=[pltpu.VMEM((nseq * (CONV_BUF + steps) * SUBLANES, LANES), F32),
                        pltpu.VMEM((nseq * steps * SUBLANES, LANES), F32)],
        compiler_params=_cparams(1),
        name="conv_sample",
    )


def _col(x, h):
    return x[:, h:h + 1]


def _with_ones(vh):
    lane = lax.broadcasted_iota(jnp.int32, (vh.shape[0], LANES), 1)
    return jnp.concatenate([vh, (lane == 0).astype(vh.dtype)], axis=1)


def _intra_head(s, vaug, b_col, r_row, inter_col, mask):
    dm = jnp.where(mask, b_col + r_row, NEG)
    m_t = jnp.maximum(inter_col, jnp.max(dm, axis=1, keepdims=True))
    sw = s * jnp.exp(dm - m_t)
    return _dot(sw.astype(BF16), vaug), m_t, jnp.exp(inter_col - m_t)


def _head_out(num_aug, m_t, gmh, so):
    h = num_aug[:, :DV] * (1.0 / jnp.maximum(jnp.abs(num_aug[:, DV:DV + 1]), jnp.exp(-m_t)))
    scale = lax.rsqrt(jnp.mean(h * h, axis=-1, keepdims=True) + EPS)
    return (h * scale * (gmh * so.astype(F32))).astype(BF16)


def _mlstm_prompt_body(q_ref, kt_ref, v_ref, so_ref, gi_ref, gf_ref, gmh_ref,
                       hmo_ref, c_ref, n_ref, m_ref, ct_ref):
    L = CHUNK
    step = pl.program_id(1)

    @pl.when(step == 0)
    def _():
        ct_ref[...] = jnp.zeros(ct_ref.shape, F32)
        m_ref[...] = jnp.zeros(m_ref.shape, F32)

    row = lax.broadcasted_iota(jnp.int32, (L, LANES), 0)
    tri = lax.broadcasted_iota(jnp.int32, (L, L), 1) <= lax.broadcasted_iota(jnp.int32, (L, L), 0)
    heads = range(M_HEADS)
    m_prev = m_ref[0]
    for t0 in range(0, q_ref.shape[0], L):
        gi = gi_ref[t0:t0 + L, :]
        b = gf_ref[t0:t0 + L, :]
        sh = 1
        while sh < L:
            b = b + jnp.where(row >= sh, pltpu.roll(b, sh, 0), 0.0)
            sh *= 2
        r = gi - b
        r_t = r.T
        b_last = b[L - 1:L, :]
        m_new = jnp.maximum(b_last + m_prev, jnp.max(b_last + r, axis=0, keepdims=True))
        dec = jnp.exp(b_last + m_prev - m_new)
        inter = b + m_prev
        qs = [q_ref[t0:t0 + L, h * DK:(h + 1) * DK] for h in heads]
        kts = [kt_ref[h * DK:(h + 1) * DK, t0:t0 + L] for h in heads]
        vaugs = [_with_ones(v_ref[t0:t0 + L, h * DV:(h + 1) * DV]) for h in heads]
        scores = [_dot(qs[h], kts[h]) for h in heads]
        ct_olds = [ct_ref[h] for h in heads]
        reads = [_dot(qs[h], ct_olds[h].astype(BF16)) for h in heads]
        for h in heads:
            wk_row = jnp.exp(_col(b_last, h) + r_t[h:h + 1, :] - _col(m_new, h))
            ct_ref[h] = _col(dec, h) * ct_olds[h] + _dot((kts[h].astype(F32) * wk_row).astype(BF16), vaugs[h])
        for h in heads:
            num_aug, m_t, w_int = _intra_head(
                scores[h], vaugs[h], _col(b, h), r_t[h:h + 1, :], _col(inter, h), tri)
            hmo_ref[t0:t0 + L, h * DV:(h + 1) * DV] = _head_out(
                num_aug + w_int * reads[h], m_t, gmh_ref[:, h * DV:(h + 1) * DV],
                so_ref[t0:t0 + L, h * DV:(h + 1) * DV])
        m_prev = m_new
    m_ref[0] = m_prev

    @pl.when(step == pl.num_programs(1) - 1)
    def _():
        for h in range(M_HEADS):
            ct = ct_ref[h]
            c_ref[0, h] = ct[:, :DV].T
            n_ref[0, h:h + 1, :] = ct[:, DV:].T[0:1, :]


def _mlstm_prompt(q, kt, v, so, gi, gf, gmh, batch, seq, depth, layer, prev):
    rows = MLSTM_CHUNKS * CHUNK
    nc = seq // rows
    n = batch * seq

    def tok(width):
        return pl.BlockSpec((rows, width), lambda b, c: (b * nc + c, 0))

    return _stacked_call(
        _mlstm_prompt_body,
        [q, kt, v, so, gi, gf, gmh],
        [tok(D_QK), pl.BlockSpec((D_QK, rows), lambda b, c: (0, b * nc + c)),
         tok(D_M), tok(D_M), tok(LANES), tok(LANES), _resident((1, D_M))],
        [1, 2], prev,
        grid=(batch, nc),
        out_specs=[tok(D_M),
                   pl.BlockSpec((None, 1, M_HEADS, DV, DK), lambda b, c: (layer, b, 0, 0, 0)),
                   pl.BlockSpec((None, 1, M_HEADS, DK), lambda b, c: (layer, b, 0, 0)),
                   pl.BlockSpec((1, 1, LANES), lambda b, c: (b, 0, 0))],
        out_shape=[jax.ShapeDtypeStruct((n, D_M), BF16),
                   jax.ShapeDtypeStruct((depth, batch, M_HEADS, DV, DK), F32),
                   jax.ShapeDtypeStruct((depth, batch, M_HEADS, DK), F32),
                   jax.ShapeDtypeStruct((batch, 1, LANES), F32)],
        scratch_shapes=[pltpu.VMEM((M_HEADS, DK, AUG), F32)],
        compiler_params=_cparams(2),
        name="mlstm_prompt",
    )


def _mlstm_sample_intra_body(steps, q_ref, k_ref, v_ref, gi_ref, gf_ref, mtok_ref,
                             num_ref, mt_ref, wi_ref, wk_ref, wkt_ref, dec_ref, mnew_ref):
    L = q_ref.shape[0]
    row = lax.broadcasted_iota(jnp.int32, (L, LANES), 0)
    pos = row & (steps - 1)

    def down(x, k):
        return pltpu.roll(x, k, 0)

    def up(x, k):
        return pltpu.roll(x, L - k, 0)

    gi = gi_ref[...]
    b = gf_ref[...]
    sh = 1
    while sh < steps:
        b = b + jnp.where(pos >= sh, down(b, sh), 0.0)
        sh *= 2
    r = gi - b
    r_t = r.T
    m_prev = mtok_ref[...]
    b_last = b
    for k in range(1, steps):
        b_last = jnp.where(pos == steps - 1 - k, up(b, k), b_last)
    d_last = b_last + r
    d_max = d_last
    sh = 1
    while sh < steps:
        d_max = jnp.maximum(d_max, jnp.where((pos & sh) == 0, up(d_max, sh), down(d_max, sh)))
        sh *= 2
    m_new = jnp.maximum(b_last + m_prev, d_max)
    wk = jnp.exp(d_last - m_new)
    inter = b + m_prev
    ti = lax.broadcasted_iota(jnp.int32, (L, L), 0)
    si = lax.broadcasted_iota(jnp.int32, (L, L), 1)
    mask = (si <= ti) & ((si & -steps) == (ti & -steps))
    lane = lax.broadcasted_iota(jnp.int32, (L, LANES), 1)

    mt_all = jnp.zeros((L, LANES), F32)
    wi_all = jnp.zeros((L, LANES), F32)
    for h in range(M_HEADS):
        s = _dot_nt(q_ref[:, h * DK:(h + 1) * DK], k_ref[:, h * DK:(h + 1) * DK])
        num_aug, m_t, w_int = _intra_head(s, _with_ones(v_ref[:, h * DV:(h + 1) * DV]),
                                          _col(b, h), r_t[h:h + 1, :], _col(inter, h), mask)
        num_ref[:, h * AUG:(h + 1) * AUG] = num_aug
        mt_all = jnp.where(lane == h, m_t, mt_all)
        wi_all = jnp.where(lane == h, w_int, wi_all)
    mt_ref[...] = mt_all
    wi_ref[...] = wi_all
    wk_ref[...] = wk
    wkt_ref[...] = wk.T[0:8, :]
    dec_ref[...] = jnp.exp(b_last + m_prev - m_new)
    mnew_ref[...] = m_new


def _mlstm_sample_intra(q, k, v, gi, gf, mtok, steps):
    n = q.shape[0]
    L = CHUNK
    scal = jax.ShapeDtypeStruct((n, LANES), F32)
    return pl.pallas_call(
        functools.partial(_mlstm_sample_intra_body, steps),
        grid=(n // L,),
        in_specs=[_rows(L, D_QK), _rows(L, D_QK), _rows(L, D_M), _rows(L, LANES), _rows(L, LANES),
                  _rows(L, LANES)],
        out_specs=[_rows(L, M_HEADS * AUG), _rows(L, LANES), _rows(L, LANES), _rows(L, LANES),
                   pl.BlockSpec((8, L), lambda i: (0, i)), _rows(L, LANES), _rows(L, LANES)],
        out_shape=[jax.ShapeDtypeStruct((n, M_HEADS * AUG), F32), scal, scal, scal,
                   jax.ShapeDtypeStruct((8, n), F32), scal, scal],
        compiler_params=_cparams(1),
        name="mlstm_sample_intra",
    )(q, k, v, gi, gf, mtok)


def _mlstm_sample_state_body(steps, q_ref, kwin_ref, vt_ref, so_ref, num_ref, mt_ref, wi_ref,
                             wk_ref, wkt_ref, dec_ref, gmh_ref, c_ref, n_ref,
                             hmo_ref, cn_ref, nn_ref):
    rows = q_ref.shape[0]
    win = kwin_ref.shape[0]
    nseq = rows // steps
    first = (pl.program_id(0) % (win // rows)) * rows
    grp = lax.broadcasted_iota(jnp.int32, (rows, 1), 0) & -steps
    lane = lax.broadcasted_iota(jnp.int32, (1, win), 1)
    col = lax.broadcasted_iota(jnp.int32, (rows, LANES), 1)
    wi = wi_ref[...]
    for h in range(M_HEADS):
        qh = q_ref[:, h * DK:(h + 1) * DK]
        qf = qh.astype(F32)
        kwin = kwin_ref[:, h * DK:(h + 1) * DK]
        k_own = kwin_ref[pl.ds(pl.multiple_of(first, rows), rows), h * DK:(h + 1) * DK].astype(F32)
        vt = vt_ref[h * DV:(h + 1) * DV, :].astype(F32)
        wk_col = _col(wk_ref[...], h)
        wk_row = wkt_ref[h:h + 1, :]
        cq = jnp.zeros((rows, DV), F32)
        nq = jnp.zeros((rows, 1), F32)
        for s in range(nseq):
            c_old = c_ref[s, h]
            n_old = n_ref[s, h:h + 1, :]
            own = grp == s * steps
            cq = jnp.where(own, _dot_nt(qh, c_old.astype(BF16)), cq)
            nq = jnp.where(own, jnp.sum(qf * n_old, axis=1, keepdims=True), nq)
            dec = dec_ref[s * steps:s * steps + 1, h:h + 1]
            lo = first + s * steps
            wrow = jnp.where((lane >= lo) & (lane < lo + steps), wk_row, 0.0)
            cn_ref[s, h] = dec * c_old + _dot((vt * wrow).astype(BF16), kwin)
            nn_ref[s, h:h + 1, :] = dec * n_old + jnp.sum(
                k_own * jnp.where(own, wk_col, 0.0), axis=0, keepdims=True)
        inter_aug = jnp.concatenate([cq, jnp.where(col == 0, nq, 0.0)], axis=1)
        num_aug = num_ref[:, h * AUG:(h + 1) * AUG] + _col(wi, h) * inter_aug
        hmo_ref[:, h * DV:(h + 1) * DV] = _head_out(
            num_aug, _col(mt_ref[...], h), gmh_ref[:, h * DV:(h + 1) * DV], so_ref[:, h * DV:(h + 1) * DV])


def _mlstm_sample_state(q, k, vt, so, num, mt, wi, wk, wkt, dec, gmh, c_all, n_all, layer, prev, steps):
    n = q.shape[0]
    batch = c_all.shape[1]
    rows = SEQ_PER_STEP * steps
    per_win = CHUNK // rows

    def win_rows(width):
        return pl.BlockSpec((CHUNK, width), lambda i: (i // per_win, 0))

    c_spec = pl.BlockSpec((None, SEQ_PER_STEP, M_HEADS, DV, DK), lambda i: (layer, i, 0, 0, 0))
    n_spec = pl.BlockSpec((None, SEQ_PER_STEP, M_HEADS, DK), lambda i: (layer, i, 0, 0))
    return _stacked_call(
        functools.partial(_mlstm_sample_state_body, steps),
        [q, k, vt, so, num, mt, wi, wk, wkt, dec, gmh, c_all, n_all],
        [_rows(rows, D_QK), win_rows(D_QK),
         pl.BlockSpec((D_M, CHUNK), lambda i: (0, i // per_win)),
         _rows(rows, D_M), _rows(rows, M_HEADS * AUG), _rows(rows, LANES),
         _rows(rows, LANES), _rows(rows, LANES),
         pl.BlockSpec((8, CHUNK), lambda i: (0, i // per_win)),
         _rows(rows, LANES), _resident((1, D_M)), c_spec, n_spec],
        [1, 2], prev,
        grid=(batch // SEQ_PER_STEP,),
        out_specs=[_rows(rows, D_M), c_spec, n_spec],
        out_shape=[jax.ShapeDtypeStruct((n, D_M), BF16),
                   jax.ShapeDtypeStruct(c_all.shape, F32),
                   jax.ShapeDtypeStruct(n_all.shape, F32)],
        compiler_params=_cparams(1),
        name="mlstm_sample_state",
    )


def _merge_body(x_ref, c_ref, hmo_ref, gc_ref, gm_ref, wco_ref, wmo_ref, wo_ref, gpm_ref, y_ref):
    y_conv = _dot(c_ref[...].astype(BF16), wco_ref[...])
    y_m = _dot(hmo_ref[...], wmo_ref[...])
    mix = gc_ref[...].astype(F32) * y_conv + gm_ref[...].astype(F32) * y_m
    y_ref[...] = x_ref[...] + _rmsnorm(_dot(mix.astype(BF16), wo_ref[...]), gpm_ref[...])


def _merge(x, c, hmo, gc, gm, wco, wmo, wo, layer, gpm, tm):
    n = x.shape[0]
    return pl.pallas_call(
        _merge_body,
        grid=(n // tm,),
        in_specs=[_rows(tm, D_MODEL), _rows(tm, D_MODEL), _rows(tm, D_M), _rows(tm, D_MODEL),
                  _rows(tm, D_MODEL),
                  _layer_of(wco, layer), _layer_of(wmo, layer), _layer_of(wo, layer),
                  _resident((1, D_MODEL))],
        out_specs=_rows(tm, D_MODEL),
        out_shape=jax.ShapeDtypeStruct((n, D_MODEL), F32),
        compiler_params=_cparams(1),
        name="merge",
    )(x, c, hmo, gc, gm, wco, wmo, wo, gpm)


def _ffn_body(x_ref, gpf_ref, w1_ref, b1_ref, w2_ref, b2_ref, gpo_ref, y_ref):
    x1 = x_ref[...]
    hf = _rmsnorm(x1, gpf_ref[...]).astype(BF16)
    a = jnp.maximum(_dot(hf, w1_ref[...]) + b1_ref[...], 0.0)
    ff = _dot((a * a).astype(BF16), w2_ref[...]) + b2_ref[...]
    y_ref[...] = x1 + _rmsnorm(ff, gpo_ref[...])


def _ffn(x, gpf, w1, b1, w2, b2, layer, gpo, tm):
    n = x.shape[0]
    return pl.pallas_call(
        _ffn_body,
        grid=(n // tm,),
        in_specs=[_rows(tm, D_MODEL), _resident((1, D_MODEL)),
                  _layer_of(w1, layer), _resident((1, D_FF)),
                  _layer_of(w2, layer), _resident((1, D_MODEL)), _resident((1, D_MODEL))],
        out_specs=_rows(tm, D_MODEL),
        out_shape=jax.ShapeDtypeStruct((n, D_MODEL), F32),
        compiler_params=_cparams(1),
        name="ffn",
    )(x, gpf, w1, b1, w2, b2, gpo)


def _row(v):
    return v.reshape(1, -1)


def _prep_weights(w_in, b_in, w_conv_out, w_m_out, w_o, w_ff1, w_ff2):
    pad = HEAD_COLS + TAIL_W - w_in.shape[2]
    return dict(w_main=jnp.pad(w_in.astype(BF16), ((0, 0), (0, 0), (0, pad))),
                b_main=jnp.pad(b_in, ((0, 0), (0, pad)))[:, None, :],
                wco=w_conv_out.astype(BF16), wmo=w_m_out.astype(BF16), wo=w_o.astype(BF16),
                w1=w_ff1.astype(BF16), w2=w_ff2.astype(BF16))


def kernel(x_prompt, x_sample, state_conv, state_C, state_n, state_m, g_pre_mix, w_in, b_in, w_dw, b_dw,
           g_cln, b_cln, w_conv_out, g_mh, w_m_out, w_o, g_post_mix, g_pre_ff, w_ff1, b_ff1, w_ff2, b_ff2,
           g_post_ff):
    bp, tp, _ = x_prompt.shape
    bs, ts, _ = x_sample.shape
    depth = w_in.shape[0]
    assert tp % (MLSTM_CHUNKS * CHUNK) == 0 and tp >= CONV_BUF and CHUNK % (SEQ_PER_STEP * ts) == 0
    assert (bs * ts) % CHUNK == 0 and bs % SEQ_PER_STEP == 0 and ts & (ts - 1) == 0
    tm_p = min(256, bp * tp)
    tm_s = min(256, bs * ts)
    assert (bp * tp) % tm_p == 0 and (bs * ts) % tm_s == 0
    yp = x_prompt.reshape(bp * tp, D_MODEL)
    ys = x_sample.reshape(bs * ts, D_MODEL)
    p = _prep_weights(w_in, b_in, w_conv_out, w_m_out, w_o, w_ff1, w_ff2)
    hist_p, m_p, m_s = [], [], []
    prompt_state = sample_state = sample_hist = None
    for d in range(depth):
        conv_args = (w_dw[d].reshape(CONV_WIDTH * SUBLANES, LANES), b_dw[d].reshape(SUBLANES, LANES),
                     _row(g_cln[d]), _row(b_cln[d]))
        gmh = _row(g_mh[d])
        inproj_args = (_row(g_pre_mix[d]), p["w_main"], p["b_main"], d)
        wk_t = w_in[d, :, K_COL:V_COL].T.astype(BF16)
        bk_t = jnp.broadcast_to(b_in[d, K_COL:V_COL].reshape(D_QK, 1), (D_QK, LANES))

        def tail(x, c, hmo, gc, gm, tm):
            x1 = _merge(x, c, hmo, gc, gm, p["wco"], p["wmo"], p["wo"], d, _row(g_post_mix[d]), tm)
            return _ffn(x1, _row(g_pre_ff[d]), p["w1"], _row(b_ff1[d]), p["w2"], _row(b_ff2[d]), d,
                        _row(g_post_ff[d]), tm)

        c, hist, q, kt, v, so, gc, gm, gi, gf = _inproj_conv(
            yp, *inproj_args, wk_t, bk_t, *conv_args, bp, tp, tm_p)
        hmo, *prompt_state, m_new = _mlstm_prompt(q, kt, v, so, gi, gf, gmh, bp, tp, depth, d, prompt_state)
        yp = tail(yp, c, hmo, gc, gm, 2 * tm_p)
        hist_p.append(hist)
        m_p.append(m_new[:, 0, :M_HEADS])

        u, q, k, v, so, gc, gm, gi, gf = _inproj(ys, *inproj_args, tm_s)
        c, *sample_hist = _conv_sample(state_conv, u, *conv_args, d, sample_hist, min(16, bs))
        mtok = jnp.pad(jnp.repeat(state_m[d], ts, axis=0), ((0, 0), (0, LANES - M_HEADS)))
        num, mt, wi, wk, wkt, dec, mnew = _mlstm_sample_intra(q, k, v, gi, gf, mtok, ts)
        hmo, *sample_state = _mlstm_sample_state(q, k, v.T, so, num, mt, wi, wk, wkt, dec, gmh,
                                                 state_C, state_n, d, sample_state, ts)
        ys = tail(ys, c, hmo, gc, gm, tm_s)
        m_s.append(mnew[ts - 1::ts, :M_HEADS])
    return (yp.reshape(bp, tp, D_MODEL), ys.reshape(bs, ts, D_MODEL),
            jnp.stack(hist_p), prompt_state[0], prompt_state[1], jnp.stack(m_p),
            sample_hist[0], sample_state[0], sample_state[1], jnp.stack(m_s))
```

```python
import functools

import jax
import jax.numpy as jnp
from jax import lax
from jax.experimental import pallas as pl
from jax.experimental.pallas import tpu as pltpu

F32 = jnp.float32
BF16 = jnp.bfloat16

D_MODEL = 1024
M_HEADS = 4
DK = 256
DV = 512
D_QK = M_HEADS * DK
D_M = M_HEADS * DV
D_FF = 4 * D_MODEL
CONV_WIDTH = 31
CONV_BUF = CONV_WIDTH - 1
CHUNK = 128
EPS = 1e-6
NEG = -1e30
LANES = 128
SUBLANES = 8
Q_COL = 2 * D_MODEL
K_COL = Q_COL + D_QK
V_COL = K_COL + D_QK
O_COL = V_COL + D_M
HEAD_COLS = O_COL + D_M
TAIL_W = -(-(2 * M_HEADS + 2 * D_MODEL) // LANES) * LANES
AUG = DV + LANES
HALO = 32
CONV_TOKENS = 8
CONV_ROWS = 16
ANCHORS_PER_SEGMENT = 4
SEQ_PER_STEP = 4
MLSTM_CHUNKS = 1

VMEM_LIMIT = 56 * 1024 * 1024


def _cparams(n_axes):
    return pltpu.CompilerParams(dimension_semantics=("arbitrary",) * n_axes,
                                vmem_limit_bytes=VMEM_LIMIT)


def _resident(shape):
    nd = len(shape)
    return pl.BlockSpec(shape, lambda *_: (0,) * nd, pipeline_mode=pl.Buffered(1))


def _layer_of(w, layer):
    tail = (0,) * (w.ndim - 1)
    return pl.BlockSpec((None,) + w.shape[1:], lambda *_: (layer,) + tail, pipeline_mode=pl.Buffered(1))


def _rows(tm, width):
    return pl.BlockSpec((tm, width), lambda i: (i, 0))


def _rmsnorm(x, g):
    return x * lax.rsqrt(jnp.mean(x * x, axis=-1, keepdims=True) + EPS) * g


def _sigmoid(x):
    return 1.0 / (1.0 + jnp.exp(-x))


def _log_sigmoid(x):
    return jnp.minimum(x, 0.0) - jnp.log(1.0 + jnp.exp(-jnp.abs(x)))


def _dot(a, b):
    return jnp.dot(a, b, preferred_element_type=F32)


def _dot_nt(a, b):
    return lax.dot_general(a, b, (((1,), (1,)), ((), ())), preferred_element_type=F32)


def _project(k_transposed, x_ref, g_ref, w_ref, b_ref, wk_ref, bk_ref,
             q_ref, k_ref, v_ref, so_ref, gc_ref, gm_ref, gi_ref, gf_ref, anchors=None):
    hb = _rmsnorm(x_ref[...], g_ref[...]).astype(BF16)

    def seg(lo, hi):
        return _dot(hb, w_ref[:, lo:hi]) + b_ref[:, lo:hi]

    u = seg(0, D_MODEL) * _sigmoid(seg(D_MODEL, Q_COL))
    q = seg(Q_COL, K_COL)
    q_ref[...] = (q * (DK ** -0.5)).astype(BF16)
    if k_transposed:
        k = _dot_nt(wk_ref[...], hb) + bk_ref[:, 0:1]
    else:
        k = seg(K_COL, V_COL)
    k_ref[...] = k.astype(BF16)
    v = seg(V_COL, O_COL)
    v_ref[...] = v.astype(BF16)
    o = seg(O_COL, HEAD_COLS)
    so_ref[...] = _sigmoid(o).astype(BF16)
    if anchors is not None:
        for res, transposed in ((q, False), (k, k_transposed), (v, False), (o, False)):
            span = res.shape[0] if transposed else res.shape[1]
            for i in range(ANCHORS_PER_SEGMENT):
                at = i * (span // ANCHORS_PER_SEGMENT)
                anchors.append(res[at:at + SUBLANES, 0:LANES] if transposed else res[0:SUBLANES, at:at + LANES])
    t = seg(HEAD_COLS, HEAD_COLS + TAIL_W)
    gi_ref[...] = t[:, :LANES]
    gf_ref[...] = _log_sigmoid(pltpu.roll(t[:, :LANES], LANES - M_HEADS, 1))
    t = pltpu.roll(t, TAIL_W - 2 * M_HEADS, 1)
    gc_ref[...] = _sigmoid(t[:, :D_MODEL]).astype(BF16)
    gm_ref[...] = _sigmoid(t[:, D_MODEL:2 * D_MODEL]).astype(BF16)
    return u


def _zero_of(x):
    bits = pltpu.bitcast(x, jnp.uint32)
    bits = lax.shift_right_logical(lax.shift_right_logical(bits, jnp.uint32(16)), jnp.uint32(16))
    return pltpu.bitcast(bits, F32)[0:1, 0:1]


def _store_tokens(buf, first_token, u):
    for c in range(SUBLANES):
        buf[pl.ds(first_token * SUBLANES + c, u.shape[0], stride=SUBLANES), :] = u[:, c * LANES:(c + 1) * LANES]


def _inproj_body(x_ref, g_ref, w_ref, b_ref, u_ref, *out_refs):
    _store_tokens(u_ref, 0, _project(False, x_ref, g_ref, w_ref, b_ref, None, None, *out_refs))


_PIECES = ((D_QK, BF16), (D_QK, BF16), (D_M, BF16), (D_M, BF16), (D_MODEL, BF16), (D_MODEL, BF16),
           (LANES, F32), (LANES, F32))


def _inproj(x, g, w, b, layer, tm):
    n = x.shape[0]
    return pl.pallas_call(
        _inproj_body,
        grid=(n // tm,),
        in_specs=[_rows(tm, D_MODEL), _resident((1, D_MODEL)), _layer_of(w, layer), _layer_of(b, layer)],
        out_specs=[_rows(tm * SUBLANES, LANES)] + [_rows(tm, wd) for wd, _ in _PIECES],
        out_shape=[jax.ShapeDtypeStruct((n * SUBLANES, LANES), F32)]
        + [jax.ShapeDtypeStruct((n, wd), dt) for wd, dt in _PIECES],
        compiler_params=_cparams(1),
        name="inproj",
    )(x, g, w, b)


def _ln_silu(c, g, b):
    mu = jnp.mean(c, axis=-1, keepdims=True)
    xc = c - mu
    y = xc * lax.rsqrt(jnp.mean(xc * xc, axis=-1, keepdims=True) + EPS) * g + b
    return y * _sigmoid(y)


def _conv_taps(win, w_ref, b_ref, n_out):
    even = [b_ref[...]] * n_out
    odd = [None] * n_out
    for j in range(CONV_WIDTH):
        wj = w_ref[j * SUBLANES:(j + 1) * SUBLANES, :]
        prods = [win[t + j] * wj for t in range(n_out)]
        if j % 2 == 0:
            even = [a + p for a, p in zip(even, prods)]
        else:
            odd = [p if a is None else a + p for a, p in zip(odd, prods)]
    return [a + b for a, b in zip(even, odd)]


def _token_rows(buf, first_token, n):
    return jnp.concatenate(
        [buf[pl.ds(first_token * SUBLANES + c, n, stride=SUBLANES), :] for c in range(SUBLANES)], axis=1)


def _inproj_conv_body(x_ref, g_ref, w_ref, b_ref, wk_ref, bk_ref, wdw_ref, bdw_ref, gln_ref, bln_ref,
                      c_ref, hist_ref, *rest):
    piece_refs, (ubuf, cbuf) = rest[:-2], rest[-2:]
    tm = x_ref.shape[0]
    halo = HALO * SUBLANES

    @pl.when(pl.program_id(1) == 0)
    def _():
        ubuf[0:halo, :] = jnp.zeros((halo, LANES), F32)

    anchors = []
    u = _project(True, x_ref, g_ref, w_ref, b_ref, wk_ref, bk_ref, *piece_refs, anchors=anchors)
    part = tm // len(anchors)
    for i, a in enumerate(anchors):
        _store_tokens(ubuf, HALO + i * part, u[i * part:(i + 1) * part, :] + _zero_of(a))
    for t0 in range(0, tm, CONV_TOKENS):
        win = [ubuf[(t0 + HALO - CONV_BUF + k) * SUBLANES:(t0 + HALO - CONV_BUF + k + 1) * SUBLANES, :]
               for k in range(CONV_TOKENS + CONV_BUF)]
        for t, acc in enumerate(_conv_taps(win, wdw_ref, bdw_ref, CONV_TOKENS)):
            cbuf[(t0 + t) * SUBLANES:(t0 + t + 1) * SUBLANES, :] = acc
    for r0 in range(0, tm, CONV_ROWS):
        c_ref[r0:r0 + CONV_ROWS, :] = _ln_silu(
            _token_rows(cbuf, r0, CONV_ROWS), gln_ref[...], bln_ref[...]).astype(c_ref.dtype)

    @pl.when(pl.program_id(1) == pl.num_programs(1) - 1)
    def _():
        hist_ref[0] = _token_rows(ubuf, HALO + tm - CONV_BUF, CONV_BUF)

    ubuf[0:halo, :] = ubuf[tm * SUBLANES:tm * SUBLANES + halo, :]


def _inproj_conv(x, g, w, b, layer, wk_t, bk_t, wdw, bdw, gln, bln, batch, seq, tm):
    nt = seq // tm
    n = batch * seq

    def tile(width):
        return pl.BlockSpec((tm, width), lambda s, i: (s * nt + i, 0))

    pieces = list(_PIECES)
    piece_specs = [tile(wd) for wd, _ in pieces]
    piece_shapes = [jax.ShapeDtypeStruct((n, wd), dt) for wd, dt in pieces]
    piece_specs[1] = pl.BlockSpec((D_QK, tm), lambda s, i: (0, s * nt + i))
    piece_shapes[1] = jax.ShapeDtypeStruct((D_QK, n), BF16)
    return pl.pallas_call(
        _inproj_conv_body,
        grid=(batch, nt),
        in_specs=[tile(D_MODEL), _resident((1, D_MODEL)), _layer_of(w, layer), _layer_of(b, layer),
                  _resident(wk_t.shape), _resident(bk_t.shape), _resident(wdw.shape), _resident(bdw.shape),
                  _resident((1, D_MODEL)), _resident((1, D_MODEL))],
        out_specs=[tile(D_MODEL), pl.BlockSpec((1, CONV_BUF, D_MODEL), lambda s, i: (s, 0, 0))] + piece_specs,
        out_shape=[jax.ShapeDtypeStruct((n, D_MODEL), BF16),
                   jax.ShapeDtypeStruct((batch, CONV_BUF, D_MODEL), F32)] + piece_shapes,
        scratch_shapes=[pltpu.VMEM(((HALO + tm) * SUBLANES, LANES), F32),
                        pltpu.VMEM((tm * SUBLANES, LANES), F32)],
        compiler_params=_cparams(2),
        name="inproj_conv",
    )(x, g, w, b, wk_t, bk_t, wdw, bdw, gln, bln)


def _conv_sample_body(steps, st_ref, u_ref, w_ref, b_ref, gln_ref, bln_ref, c_ref, hist_ref, fbuf, cbuf):
    nseq = st_ref.shape[0]
    full = CONV_BUF + steps
    for s in range(nseq):
        f0 = s * full
        for c in range(SUBLANES):
            fbuf[pl.ds(f0 * SUBLANES + c, CONV_BUF, stride=SUBLANES), :] = st_ref[s, :, c * LANES:(c + 1) * LANES]
        fbuf[(f0 + CONV_BUF) * SUBLANES:(f0 + full) * SUBLANES, :] = (
            u_ref[s * steps * SUBLANES:(s + 1) * steps * SUBLANES, :])
        win = [fbuf[(f0 + k) * SUBLANES:(f0 + k + 1) * SUBLANES, :] for k in range(full)]
        for t, acc in enumerate(_conv_taps(win, w_ref, b_ref, steps)):
            cbuf[(s * steps + t) * SUBLANES:(s * steps + t + 1) * SUBLANES, :] = acc
        hist_ref[s] = _token_rows(fbuf, f0 + steps, CONV_BUF)
    c_ref[...] = _ln_silu(_token_rows(cbuf, 0, nseq * steps), gln_ref[...], bln_ref[...])


def _drop_refs(body, first, count, *refs):
    return body(*refs[:first], *refs[first + count:])


def _stacked_call(body, args, in_specs, stacked_outs, prev, **kwargs):
    aliases = {}
    if prev is not None:
        first = len(args)
        in_specs = list(in_specs) + [pl.BlockSpec(memory_space=pl.ANY)] * len(prev)
        aliases = {first + j: out for j, out in enumerate(stacked_outs)}
        args = list(args) + list(prev)
        body = functools.partial(_drop_refs, body, first, len(prev))
    return pl.pallas_call(body, in_specs=in_specs, input_output_aliases=aliases, **kwargs)(*args)


def _conv_sample(state_all, u, wdw, bdw, gln, bln, layer, prev, nseq):
    batch = state_all.shape[1]
    steps = u.shape[0] // (batch * SUBLANES)
    hist_spec = pl.BlockSpec((None, nseq, CONV_BUF, D_MODEL), lambda i: (layer, i, 0, 0))
    return _stacked_call(
        functools.partial(_conv_sample_body, steps),
        [state_all, u, wdw, bdw, gln, bln],
        [hist_spec, _rows(nseq * steps * SUBLANES, LANES), _resident(wdw.shape), _resident(bdw.shape),
         _resident((1, D_MODEL)), _resident((1, D_MODEL))],
        [1], prev,
        grid=(batch // nseq,),
        out_specs=[_rows(nseq * steps, D_MODEL), hist_spec],
        out_shape=[jax.ShapeDtypeStruct((batch * steps, D_MODEL), F32),
                   jax.ShapeDtypeStruct(state_all.shape, F32)],
        scratch_shapes=[pltpu.VMEM((nseq * (CONV_BUF + steps) * SUBLANES, LANES), F32),
                        pltpu.VMEM((nseq * steps * SUBLANES, LANES), F32)],
        compiler_params=_cparams(1),
        name="conv_sample",
    )


def _col(x, h):
    return x[:, h:h + 1]


def _with_ones(vh):
    lane = lax.broadcasted_iota(jnp.int32, (vh.shape[0], LANES), 1)
    return jnp.concatenate([vh, (lane == 0).astype(vh.dtype)], axis=1)


def _intra_head(s, vaug, b_col, r_row, inter_col, mask):
    dm = jnp.where(mask, b_col + r_row, NEG)
    m_t = jnp.maximum(inter_col, jnp.max(dm, axis=1, keepdims=True))
    sw = s * jnp.exp(dm - m_t)
    return _dot(sw.astype(BF16), vaug), m_t, jnp.exp(inter_col - m_t)


def _head_out(num_aug, m_t, gmh, so):
    h = num_aug[:, :DV] * (1.0 / jnp.maximum(jnp.abs(num_aug[:, DV:DV + 1]), jnp.exp(-m_t)))
    scale = lax.rsqrt(jnp.mean(h * h, axis=-1, keepdims=True) + EPS)
    return (h * scale * (gmh * so.astype(F32))).astype(BF16)


def _mlstm_prompt_body(q_ref, kt_ref, v_ref, so_ref, gi_ref, gf_ref, gmh_ref,
                       hmo_ref, c_ref, n_ref, m_ref, ct_ref):
    L = CHUNK
    step = pl.program_id(1)

    @pl.when(step == 0)
    def _():
        ct_ref[...] = jnp.zeros(ct_ref.shape, F32)
        m_ref[...] = jnp.zeros(m_ref.shape, F32)

    row = lax.broadcasted_iota(jnp.int32, (L, LANES), 0)
    tri = lax.broadcasted_iota(jnp.int32, (L, L), 1) <= lax.broadcasted_iota(jnp.int32, (L, L), 0)
    heads = range(M_HEADS)
    m_prev = m_ref[0]
    for t0 in range(0, q_ref.shape[0], L):
        gi = gi_ref[t0:t0 + L, :]
        b = gf_ref[t0:t0 + L, :]
        sh = 1
        while sh < L:
            b = b + jnp.where(row >= sh, pltpu.roll(b, sh, 0), 0.0)
            sh *= 2
        r = gi - b
        r_t = r.T
        b_last = b[L - 1:L, :]
        m_new = jnp.maximum(b_last + m_prev, jnp.max(b_last + r, axis=0, keepdims=True))
        dec = jnp.exp(b_last + m_prev - m_new)
        inter = b + m_prev
        qs = [q_ref[t0:t0 + L, h * DK:(h + 1) * DK] for h in heads]
        kts = [kt_ref[h * DK:(h + 1) * DK, t0:t0 + L] for h in heads]
        vaugs = [_with_ones(v_ref[t0:t0 + L, h * DV:(h + 1) * DV]) for h in heads]
        scores = [_dot(qs[h], kts[h]) for h in heads]
        ct_olds = [ct_ref[h] for h in heads]
        reads = [_dot(qs[h], ct_olds[h].astype(BF16)) for h in heads]
        for h in heads:
            wk_row = jnp.exp(_col(b_last, h) + r_t[h:h + 1, :] - _col(m_new, h))
            ct_ref[h] = _col(dec, h) * ct_olds[h] + _dot((kts[h].astype(F32) * wk_row).astype(BF16), vaugs[h])
        for h in heads:
            num_aug, m_t, w_int = _intra_head(
                scores[h], vaugs[h], _col(b, h), r_t[h:h + 1, :], _col(inter, h), tri)
            hmo_ref[t0:t0 + L, h * DV:(h + 1) * DV] = _head_out(
                num_aug + w_int * reads[h], m_t, gmh_ref[:, h * DV:(h + 1) * DV],
                so_ref[t0:t0 + L, h * DV:(h + 1) * DV])
        m_prev = m_new
    m_ref[0] = m_prev

    @pl.when(step == pl.num_programs(1) - 1)
    def _():
        for h in range(M_HEADS):
            ct = ct_ref[h]
            c_ref[0, h] = ct[:, :DV].T
            n_ref[0, h:h + 1, :] = ct[:, DV:].T[0:1, :]


def _mlstm_prompt(q, kt, v, so, gi, gf, gmh, batch, seq, depth, layer, prev):
    rows = MLSTM_CHUNKS * CHUNK
    nc = seq // rows
    n = batch * seq

    def tok(width):
        return pl.BlockSpec((rows, width), lambda b, c: (b * nc + c, 0))

    return _stacked_call(
        _mlstm_prompt_body,
        [q, kt, v, so, gi, gf, gmh],
        [tok(D_QK), pl.BlockSpec((D_QK, rows), lambda b, c: (0, b * nc + c)),
         tok(D_M), tok(D_M), tok(LANES), tok(LANES), _resident((1, D_M))],
        [1, 2], prev,
        grid=(batch, nc),
        out_specs=[tok(D_M),
                   pl.BlockSpec((None, 1, M_HEADS, DV, DK), lambda b, c: (layer, b, 0, 0, 0)),
                   pl.BlockSpec((None, 1, M_HEADS, DK), lambda b, c: (layer, b, 0, 0)),
                   pl.BlockSpec((1, 1, LANES), lambda b, c: (b, 0, 0))],
        out_shape=[jax.ShapeDtypeStruct((n, D_M), BF16),
                   jax.ShapeDtypeStruct((depth, batch, M_HEADS, DV, DK), F32),
                   jax.ShapeDtypeStruct((depth, batch, M_HEADS, DK), F32),
                   jax.ShapeDtypeStruct((batch, 1, LANES), F32)],
        scratch_shapes=[pltpu.VMEM((M_HEADS, DK, AUG), F32)],
        compiler_params=_cparams(2),
        name="mlstm_prompt",
    )


def _mlstm_sample_intra_body(steps, q_ref, k_ref, v_ref, gi_ref, gf_ref, mtok_ref,
                             num_ref, mt_ref, wi_ref, wk_ref, wkt_ref, dec_ref, mnew_ref):
    L = q_ref.shape[0]
    row = lax.broadcasted_iota(jnp.int32, (L, LANES), 0)
    pos = row & (steps - 1)

    def down(x, k):
        return pltpu.roll(x, k, 0)

    def up(x, k):
        return pltpu.roll(x, L - k, 0)

    gi = gi_ref[...]
    b = gf_ref[...]
    sh = 1
    while sh < steps:
        b = b + jnp.where(pos >= sh, down(b, sh), 0.0)
        sh *= 2
    r = gi - b
    r_t = r.T
    m_prev = mtok_ref[...]
    b_last = b
    for k in range(1, steps):
        b_last = jnp.where(pos == steps - 1 - k, up(b, k), b_last)
    d_last = b_last + r
    d_max = d_last
    sh = 1
    while sh < steps:
        d_max = jnp.maximum(d_max, jnp.where((pos & sh) == 0, up(d_max, sh), down(d_max, sh)))
        sh *= 2
    m_new = jnp.maximum(b_last + m_prev, d_max)
    wk = jnp.exp(d_last - m_new)
    inter = b + m_prev
    ti = lax.broadcasted_iota(jnp.int32, (L, L), 0)
    si = lax.broadcasted_iota(jnp.int32, (L, L), 1)
    mask = (si <= ti) & ((si & -steps) == (ti & -steps))
    lane = lax.broadcasted_iota(jnp.int32, (L, LANES), 1)

    mt_all = jnp.zeros((L, LANES), F32)
    wi_all = jnp.zeros((L, LANES), F32)
    for h in range(M_HEADS):
        s = _dot_nt(q_ref[:, h * DK:(h + 1) * DK], k_ref[:, h * DK:(h + 1) * DK])
        num_aug, m_t, w_int = _intra_head(s, _with_ones(v_ref[:, h * DV:(h + 1) * DV]),
                                          _col(b, h), r_t[h:h + 1, :], _col(inter, h), mask)
        num_ref[:, h * AUG:(h + 1) * AUG] = num_aug
        mt_all = jnp.where(lane == h, m_t, mt_all)
        wi_all = jnp.where(lane == h, w_int, wi_all)
    mt_ref[...] = mt_all
    wi_ref[...] = wi_all
    wk_ref[...] = wk
    wkt_ref[...] = wk.T[0:8, :]
    dec_ref[...] = jnp.exp(b_last + m_prev - m_new)
    mnew_ref[...] = m_new


def _mlstm_sample_intra(q, k, v, gi, gf, mtok, steps):
    n = q.shape[0]
    L = CHUNK
    scal = jax.ShapeDtypeStruct((n, LANES), F32)
    return pl.pallas_call(
        functools.partial(_mlstm_sample_intra_body, steps),
        grid=(n // L,),
        in_specs=[_rows(L, D_QK), _rows(L, D_QK), _rows(L, D_M), _rows(L, LANES), _rows(L, LANES),
                  _rows(L, LANES)],
        out_specs=[_rows(L, M_HEADS * AUG), _rows(L, LANES), _rows(L, LANES), _rows(L, LANES),
                   pl.BlockSpec((8, L), lambda i: (0, i)), _rows(L, LANES), _rows(L, LANES)],
        out_shape=[jax.ShapeDtypeStruct((n, M_HEADS * AUG), F32), scal, scal, scal,
                   jax.ShapeDtypeStruct((8, n), F32), scal, scal],
        compiler_params=_cparams(1),
        name="mlstm_sample_intra",
    )(q, k, v, gi, gf, mtok)


def _mlstm_sample_state_body(steps, q_ref, kwin_ref, vt_ref, so_ref, num_ref, mt_ref, wi_ref,
                             wk_ref, wkt_ref, dec_ref, gmh_ref, c_ref, n_ref,
                             hmo_ref, cn_ref, nn_ref):
    rows = q_ref.shape[0]
    win = kwin_ref.shape[0]
    nseq = rows // steps
    first = (pl.program_id(0) % (win // rows)) * rows
    grp = lax.broadcasted_iota(jnp.int32, (rows, 1), 0) & -steps
    lane = lax.broadcasted_iota(jnp.int32, (1, win), 1)
    col = lax.broadcasted_iota(jnp.int32, (rows, LANES), 1)
    wi = wi_ref[...]
    for h in range(M_HEADS):
        qh = q_ref[:, h * DK:(h + 1) * DK]
        qf = qh.astype(F32)
        kwin = kwin_ref[:, h * DK:(h + 1) * DK]
        k_own = kwin_ref[pl.ds(pl.multiple_of(first, rows), rows), h * DK:(h + 1) * DK].astype(F32)
        vt = vt_ref[h * DV:(h + 1) * DV, :].astype(F32)
        wk_col = _col(wk_ref[...], h)
        wk_row = wkt_ref[h:h + 1, :]
        cq = jnp.zeros((rows, DV), F32)
        nq = jnp.zeros((rows, 1), F32)
        for s in range(nseq):
            c_old = c_ref[s, h]
            n_old = n_ref[s, h:h + 1, :]
            own = grp == s * steps
            cq = jnp.where(own, _dot_nt(qh, c_old.astype(BF16)), cq)
            nq = jnp.where(own, jnp.sum(qf * n_old, axis=1, keepdims=True), nq)
            dec = dec_ref[s * steps:s * steps + 1, h:h + 1]
            lo = first + s * steps
            wrow = jnp.where((lane >= lo) & (lane < lo + steps), wk_row, 0.0)
            cn_ref[s, h] = dec * c_old + _dot((vt * wrow).astype(BF16), kwin)
            nn_ref[s, h:h + 1, :] = dec * n_old + jnp.sum(
                k_own * jnp.where(own, wk_col, 0.0), axis=0, keepdims=True)
        inter_aug = jnp.concatenate([cq, jnp.where(col == 0, nq, 0.0)], axis=1)
        num_aug = num_ref[:, h * AUG:(h + 1) * AUG] + _col(wi, h) * inter_aug
        hmo_ref[:, h * DV:(h + 1) * DV] = _head_out(
            num_aug, _col(mt_ref[...], h), gmh_ref[:, h * DV:(h + 1) * DV], so_ref[:, h * DV:(h + 1) * DV])


def _mlstm_sample_state(q, k, vt, so, num, mt, wi, wk, wkt, dec, gmh, c_all, n_all, layer, prev, steps):
    n = q.shape[0]
    batch = c_all.shape[1]
    rows = SEQ_PER_STEP * steps
    per_win = CHUNK // rows

    def win_rows(width):
        return pl.BlockSpec((CHUNK, width), lambda i: (i // per_win, 0))

    c_spec = pl.BlockSpec((None, SEQ_PER_STEP, M_HEADS, DV, DK), lambda i: (layer, i, 0, 0, 0))
    n_spec = pl.BlockSpec((None, SEQ_PER_STEP, M_HEADS, DK), lambda i: (layer, i, 0, 0))
    return _stacked_call(
        functools.partial(_mlstm_sample_state_body, steps),
        [q, k, vt, so, num, mt, wi, wk, wkt, dec, gmh, c_all, n_all],
        [_rows(rows, D_QK), win_rows(D_QK),
         pl.BlockSpec((D_M, CHUNK), lambda i: (0, i // per_win)),
         _rows(rows, D_M), _rows(rows, M_HEADS * AUG), _rows(rows, LANES),
         _rows(rows, LANES), _rows(rows, LANES),
         pl.BlockSpec((8, CHUNK), lambda i: (0, i // per_win)),
         _rows(rows, LANES), _resident((1, D_M)), c_spec, n_spec],
        [1, 2], prev,
        grid=(batch // SEQ_PER_STEP,),
        out_specs=[_rows(rows, D_M), c_spec, n_spec],
        out_shape=[jax.ShapeDtypeStruct((n, D_M), BF16),
                   jax.ShapeDtypeStruct(c_all.shape, F32),
                   jax.ShapeDtypeStruct(n_all.shape, F32)],
        compiler_params=_cparams(1),
        name="mlstm_sample_state",
    )


def _merge_body(x_ref, c_ref, hmo_ref, gc_ref, gm_ref, wco_ref, wmo_ref, wo_ref, gpm_ref, y_ref):
    y_conv = _dot(c_ref[...].astype(BF16), wco_ref[...])
    y_m = _dot(hmo_ref[...], wmo_ref[...])
    mix = gc_ref[...].astype(F32) * y_conv + gm_ref[...].astype(F32) * y_m
    y_ref[...] = x_ref[...] + _rmsnorm(_dot(mix.astype(BF16), wo_ref[...]), gpm_ref[...])


def _merge(x, c, hmo, gc, gm, wco, wmo, wo, layer, gpm, tm):
    n = x.shape[0]
    return pl.pallas_call(
        _merge_body,
        grid=(n // tm,),
        in_specs=[_rows(tm, D_MODEL), _rows(tm, D_MODEL), _rows(tm, D_M), _rows(tm, D_MODEL),
                  _rows(tm, D_MODEL),
                  _layer_of(wco, layer), _layer_of(wmo, layer), _layer_of(wo, layer),
                  _resident((1, D_MODEL))],
        out_specs=_rows(tm, D_MODEL),
        out_shape=jax.ShapeDtypeStruct((n, D_MODEL), F32),
        compiler_params=_cparams(1),
        name="merge",
    )(x, c, hmo, gc, gm, wco, wmo, wo, gpm)


def _ffn_body(x_ref, gpf_ref, w1_ref, b1_ref, w2_ref, b2_ref, gpo_ref, y_ref):
    x1 = x_ref[...]
    hf = _rmsnorm(x1, gpf_ref[...]).astype(BF16)
    a = jnp.maximum(_dot(hf, w1_ref[...]) + b1_ref[...], 0.0)
    ff = _dot((a * a).astype(BF16), w2_ref[...]) + b2_ref[...]
    y_ref[...] = x1 + _rmsnorm(ff, gpo_ref[...])


def _ffn(x, gpf, w1, b1, w2, b2, layer, gpo, tm):
    n = x.shape[0]
    return pl.pallas_call(
        _ffn_body,
        grid=(n // tm,),
        in_specs=[_rows(tm, D_MODEL), _resident((1, D_MODEL)),
                  _layer_of(w1, layer), _resident((1, D_FF)),
                  _layer_of(w2, layer), _resident((1, D_MODEL)), _resident((1, D_MODEL))],
        out_specs=_rows(tm, D_MODEL),
        out_shape=jax.ShapeDtypeStruct((n, D_MODEL), F32),
        compiler_params=_cparams(1),
        name="ffn",
    )(x, gpf, w1, b1, w2, b2, gpo)


def _row(v):
    return v.reshape(1, -1)


def _prep_weights(w_in, b_in, w_conv_out, w_m_out, w_o, w_ff1, w_ff2):
    pad = HEAD_COLS + TAIL_W - w_in.shape[2]
    return dict(w_main=jnp.pad(w_in.astype(BF16), ((0, 0), (0, 0), (0, pad))),
                b_main=jnp.pad(b_in, ((0, 0), (0, pad)))[:, None, :],
                wco=w_conv_out.astype(BF16), wmo=w_m_out.astype(BF16), wo=w_o.astype(BF16),
                w1=w_ff1.astype(BF16), w2=w_ff2.astype(BF16))


def kernel(x_prompt, x_sample, state_conv, state_C, state_n, state_m, g_pre_mix, w_in, b_in, w_dw, b_dw,
           g_cln, b_cln, w_conv_out, g_mh, w_m_out, w_o, g_post_mix, g_pre_ff, w_ff1, b_ff1, w_ff2, b_ff2,
           g_post_ff):
    bp, tp, _ = x_prompt.shape
    bs, ts, _ = x_sample.shape
    depth = w_in.shape[0]
    assert tp % (MLSTM_CHUNKS * CHUNK) == 0 and tp >= CONV_BUF and CHUNK % (SEQ_PER_STEP * ts) == 0
    assert (bs * ts) % CHUNK == 0 and bs % SEQ_PER_STEP == 0 and ts & (ts - 1) == 0
    tm_p = min(256, bp * tp)
    tm_s = min(256, bs * ts)
    assert (bp * tp) % tm_p == 0 and (bs * ts) % tm_s == 0
    yp = x_prompt.reshape(bp * tp, D_MODEL)
    ys = x_sample.reshape(bs * ts, D_MODEL)
    p = _prep_weights(w_in, b_in, w_conv_out, w_m_out, w_o, w_ff1, w_ff2)
    hist_p, m_p, m_s = [], [], []
    prompt_state = sample_state = sample_hist = None
    for d in range(depth):
        conv_args = (w_dw[d].reshape(CONV_WIDTH * SUBLANES, LANES), b_dw[d].reshape(SUBLANES, LANES),
                     _row(g_cln[d]), _row(b_cln[d]))
        gmh = _row(g_mh[d])
        inproj_args = (_row(g_pre_mix[d]), p["w_main"], p["b_main"], d)
        wk_t = w_in[d, :, K_COL:V_COL].T.astype(BF16)
        bk_t = jnp.broadcast_to(b_in[d, K_COL:V_COL].reshape(D_QK, 1), (D_QK, LANES))

        def tail(x, c, hmo, gc, gm, tm):
            x1 = _merge(x, c, hmo, gc, gm, p["wco"], p["wmo"], p["wo"], d, _row(g_post_mix[d]), tm)
            return _ffn(x1, _row(g_pre_ff[d]), p["w1"], _row(b_ff1[d]), p["w2"], _row(b_ff2[d]), d,
                        _row(g_post_ff[d]), tm)

        c, hist, q, kt, v, so, gc, gm, gi, gf = _inproj_conv(
            yp, *inproj_args, wk_t, bk_t, *conv_args, bp, tp, tm_p)
        hmo, *prompt_state, m_new = _mlstm_prompt(q, kt, v, so, gi, gf, gmh, bp, tp, depth, d, prompt_state)
        yp = tail(yp, c, hmo, gc, gm, 2 * tm_p)
        hist_p.append(hist)
        m_p.append(m_new[:, 0, :M_HEADS])

        u, q, k, v, so, gc, gm, gi, gf = _inproj(ys, *inproj_args, tm_s)
        c, *sample_hist = _conv_sample(state_conv, u, *conv_args, d, sample_hist, min(16, bs))
        mtok = jnp.pad(jnp.repeat(state_m[d], ts, axis=0), ((0, 0), (0, LANES - M_HEADS)))
        num, mt, wi, wk, wkt, dec, mnew = _mlstm_sample_intra(q, k, v, gi, gf, mtok, ts)
        hmo, *sample_state = _mlstm_sample_state(q, k, v.T, so, num, mt, wi, wk, wkt, dec, gmh,
                                                 state_C, state_n, d, sample_state, ts)
        ys = tail(ys, c, hmo, gc, gm, tm_s)
        m_s.append(mnew[ts - 1::ts, :M_HEADS])
    return (yp.reshape(bp, tp, D_MODEL), ys.reshape(bs, ts, D_MODEL),
            jnp.stack(hist_p), prompt_state[0], prompt_state[1], jnp.stack(m_p),
            sample_hist[0], sample_state[0], sample_state[1], jnp.stack(m_s))
```

```python
import functools

import jax
import jax.numpy as jnp
from jax import lax
from jax.experimental import pallas as pl
from jax.experimental.pallas import tpu as pltpu

F32 = jnp.float32
BF16 = jnp.bfloat16

D_MODEL = 1024
M_HEADS = 4
DK = 256
DV = 512
D_QK = M_HEADS * DK
D_M = M_HEADS * DV
D_FF = 4 * D_MODEL
CONV_WIDTH = 31
CONV_BUF = CONV_WIDTH - 1
CHUNK = 128
EPS = 1e-6
NEG = -1e30
LANES = 128
SUBLANES = 8
Q_COL = 2 * D_MODEL
K_COL = Q_COL + D_QK
V_COL = K_COL + D_QK
O_COL = V_COL + D_M
HEAD_COLS = O_COL + D_M
TAIL_W = -(-(2 * M_HEADS + 2 * D_MODEL) // LANES) * LANES
AUG = DV + LANES
HALO = 32
CONV_TOKENS = 8
CONV_ROWS = 16
ANCHORS_PER_SEGMENT = 2
SEQ_PER_STEP = 4
MLSTM_CHUNKS = 1

VMEM_LIMIT = 56 * 1024 * 1024


def _cparams(n_axes):
    return pltpu.CompilerParams(dimension_semantics=("arbitrary",) * n_axes,
                                vmem_limit_bytes=VMEM_LIMIT)


def _resident(shape):
    nd = len(shape)
    return pl.BlockSpec(shape, lambda *_: (0,) * nd, pipeline_mode=pl.Buffered(1))


def _layer_of(w, layer):
    tail = (0,) * (w.ndim - 1)
    return pl.BlockSpec((None,) + w.shape[1:], lambda *_: (layer,) + tail, pipeline_mode=pl.Buffered(1))


def _rows(tm, width):
    return pl.BlockSpec((tm, width), lambda i: (i, 0))


def _rmsnorm(x, g):
    return x * lax.rsqrt(jnp.mean(x * x, axis=-1, keepdims=True) + EPS) * g


def _sigmoid(x):
    return 1.0 / (1.0 + jnp.exp(-x))


def _log_sigmoid(x):
    return jnp.minimum(x, 0.0) - jnp.log(1.0 + jnp.exp(-jnp.abs(x)))


def _dot(a, b):
    return jnp.dot(a, b, preferred_element_type=F32)


def _dot_nt(a, b):
    return lax.dot_general(a, b, (((1,), (1,)), ((), ())), preferred_element_type=F32)


def _project(k_transposed, x_ref, g_ref, w_ref, b_ref, wk_ref, bk_ref,
             q_ref, k_ref, v_ref, so_ref, gc_ref, gm_ref, gi_ref, gf_ref, anchors=None):
    hb = _rmsnorm(x_ref[...], g_ref[...]).astype(BF16)

    def seg(lo, hi):
        return _dot(hb, w_ref[:, lo:hi]) + b_ref[:, lo:hi]

    u = seg(0, D_MODEL) * _sigmoid(seg(D_MODEL, Q_COL))
    q = seg(Q_COL, K_COL)
    q_ref[...] = (q * (DK ** -0.5)).astype(BF16)
    if k_transposed:
        k = _dot_nt(wk_ref[...], hb) + bk_ref[:, 0:1]
    else:
        k = seg(K_COL, V_COL)
    k_ref[...] = k.astype(BF16)
    v = seg(V_COL, O_COL)
    v_ref[...] = v.astype(BF16)
    o = seg(O_COL, HEAD_COLS)
    so_ref[...] = _sigmoid(o).astype(BF16)
    if anchors is not None:
        for res, transposed in ((q, False), (k, k_transposed), (v, False), (o, False)):
            span = res.shape[0] if transposed else res.shape[1]
            for i in range(ANCHORS_PER_SEGMENT):
                at = i * (span // ANCHORS_PER_SEGMENT)
                anchors.append(res[at:at + SUBLANES, 0:LANES] if transposed else res[0:SUBLANES, at:at + LANES])
    t = seg(HEAD_COLS, HEAD_COLS + TAIL_W)
    gi_ref[...] = t[:, :LANES]
    gf_ref[...] = _log_sigmoid(pltpu.roll(t[:, :LANES], LANES - M_HEADS, 1))
    t = pltpu.roll(t, TAIL_W - 2 * M_HEADS, 1)
    gc_ref[...] = _sigmoid(t[:, :D_MODEL]).astype(BF16)
    gm_ref[...] = _sigmoid(t[:, D_MODEL:2 * D_MODEL]).astype(BF16)
    return u


def _zero_of(x):
    bits = pltpu.bitcast(x, jnp.uint32)
    bits = lax.shift_right_logical(lax.shift_right_logical(bits, jnp.uint32(16)), jnp.uint32(16))
    return pltpu.bitcast(bits, F32)[0:1, 0:1]


def _store_tokens(buf, first_token, u):
    for c in range(SUBLANES):
        buf[pl.ds(first_token * SUBLANES + c, u.shape[0], stride=SUBLANES), :] = u[:, c * LANES:(c + 1) * LANES]


def _inproj_body(x_ref, g_ref, w_ref, b_ref, u_ref, *out_refs):
    _store_tokens(u_ref, 0, _project(False, x_ref, g_ref, w_ref, b_ref, None, None, *out_refs))


_PIECES = ((D_QK, BF16), (D_QK, BF16), (D_M, BF16), (D_M, BF16), (D_MODEL, BF16), (D_MODEL, BF16),
           (LANES, F32), (LANES, F32))


def _inproj(x, g, w, b, layer, tm):
    n = x.shape[0]
    return pl.pallas_call(
        _inproj_body,
        grid=(n // tm,),
        in_specs=[_rows(tm, D_MODEL), _resident((1, D_MODEL)), _layer_of(w, layer), _layer_of(b, layer)],
        out_specs=[_rows(tm * SUBLANES, LANES)] + [_rows(tm, wd) for wd, _ in _PIECES],
        out_shape=[jax.ShapeDtypeStruct((n * SUBLANES, LANES), F32)]
        + [jax.ShapeDtypeStruct((n, wd), dt) for wd, dt in _PIECES],
        compiler_params=_cparams(1),
        name="inproj",
    )(x, g, w, b)


def _ln_silu(c, g, b):
    mu = jnp.mean(c, axis=-1, keepdims=True)
    xc = c - mu
    y = xc * lax.rsqrt(jnp.mean(xc * xc, axis=-1, keepdims=True) + EPS) * g + b
    return y * _sigmoid(y)


def _conv_taps(win, w_ref, b_ref, n_out):
    even = [b_ref[...]] * n_out
    odd = [None] * n_out
    for j in range(CONV_WIDTH):
        wj = w_ref[j * SUBLANES:(j + 1) * SUBLANES, :]
        prods = [win[t + j] * wj for t in range(n_out)]
        if j % 2 == 0:
            even = [a + p for a, p in zip(even, prods)]
        else:
            odd = [p if a is None else a + p for a, p in zip(odd, prods)]
    return [a + b for a, b in zip(even, odd)]


def _token_rows(buf, first_token, n):
    return jnp.concatenate(
        [buf[pl.ds(first_token * SUBLANES + c, n, stride=SUBLANES), :] for c in range(SUBLANES)], axis=1)


def _inproj_conv_body(x_ref, g_ref, w_ref, b_ref, wk_ref, bk_ref, wdw_ref, bdw_ref, gln_ref, bln_ref,
                      c_ref, hist_ref, *rest):
    piece_refs, (ubuf, cbuf) = rest[:-2], rest[-2:]
    tm = x_ref.shape[0]
    halo = HALO * SUBLANES

    @pl.when(pl.program_id(1) == 0)
    def _():
        ubuf[0:halo, :] = jnp.zeros((halo, LANES), F32)

    anchors = []
    u = _project(True, x_ref, g_ref, w_ref, b_ref, wk_ref, bk_ref, *piece_refs, anchors=anchors)
    part = tm // len(anchors)
    for i, a in enumerate(anchors):
        _store_tokens(ubuf, HALO + i * part, u[i * part:(i + 1) * part, :] + _zero_of(a))
    for t0 in range(0, tm, CONV_TOKENS):
        win = [ubuf[(t0 + HALO - CONV_BUF + k) * SUBLANES:(t0 + HALO - CONV_BUF + k + 1) * SUBLANES, :]
               for k in range(CONV_TOKENS + CONV_BUF)]
        for t, acc in enumerate(_conv_taps(win, wdw_ref, bdw_ref, CONV_TOKENS)):
            cbuf[(t0 + t) * SUBLANES:(t0 + t + 1) * SUBLANES, :] = acc
    for r0 in range(0, tm, CONV_ROWS):
        c_ref[r0:r0 + CONV_ROWS, :] = _ln_silu(
            _token_rows(cbuf, r0, CONV_ROWS), gln_ref[...], bln_ref[...]).astype(c_ref.dtype)

    @pl.when(pl.program_id(1) == pl.num_programs(1) - 1)
    def _():
        hist_ref[0] = _token_rows(ubuf, HALO + tm - CONV_BUF, CONV_BUF)

    ubuf[0:halo, :] = ubuf[tm * SUBLANES:tm * SUBLANES + halo, :]


def _inproj_conv(x, g, w, b, layer, wk_t, bk_t, wdw, bdw, gln, bln, batch, seq, tm):
    nt = seq // tm
    n = batch * seq

    def tile(width):
        return pl.BlockSpec((tm, width), lambda s, i: (s * nt + i, 0))

    pieces = list(_PIECES)
    piece_specs = [tile(wd) for wd, _ in pieces]
    piece_shapes = [jax.ShapeDtypeStruct((n, wd), dt) for wd, dt in pieces]
    piece_specs[1] = pl.BlockSpec((D_QK, tm), lambda s, i: (0, s * nt + i))
    piece_shapes[1] = jax.ShapeDtypeStruct((D_QK, n), BF16)
    return pl.pallas_call(
        _inproj_conv_body,
        grid=(batch, nt),
        in_specs=[tile(D_MODEL), _resident((1, D_MODEL)), _layer_of(w, layer), _layer_of(b, layer),
                  _resident(wk_t.shape), _resident(bk_t.shape), _resident(wdw.shape), _resident(bdw.shape),
                  _resident((1, D_MODEL)), _resident((1, D_MODEL))],
        out_specs=[tile(D_MODEL), pl.BlockSpec((1, CONV_BUF, D_MODEL), lambda s, i: (s, 0, 0))] + piece_specs,
        out_shape=[jax.ShapeDtypeStruct((n, D_MODEL), BF16),
                   jax.ShapeDtypeStruct((batch, CONV_BUF, D_MODEL), F32)] + piece_shapes,
        scratch_shapes=[pltpu.VMEM(((HALO + tm) * SUBLANES, LANES), F32),
                        pltpu.VMEM((tm * SUBLANES, LANES), F32)],
        compiler_params=_cparams(2),
        name="inproj_conv",
    )(x, g, w, b, wk_t, bk_t, wdw, bdw, gln, bln)


def _conv_sample_body(steps, st_ref, u_ref, w_ref, b_ref, gln_ref, bln_ref, c_ref, hist_ref, fbuf, cbuf):
    nseq = st_ref.shape[0]
    full = CONV_BUF + steps
    for s in range(nseq):
        f0 = s * full
        for c in range(SUBLANES):
            fbuf[pl.ds(f0 * SUBLANES + c, CONV_BUF, stride=SUBLANES), :] = st_ref[s, :, c * LANES:(c + 1) * LANES]
        fbuf[(f0 + CONV_BUF) * SUBLANES:(f0 + full) * SUBLANES, :] = (
            u_ref[s * steps * SUBLANES:(s + 1) * steps * SUBLANES, :])
        win = [fbuf[(f0 + k) * SUBLANES:(f0 + k + 1) * SUBLANES, :] for k in range(full)]
        for t, acc in enumerate(_conv_taps(win, w_ref, b_ref, steps)):
            cbuf[(s * steps + t) * SUBLANES:(s * steps + t + 1) * SUBLANES, :] = acc
        hist_ref[s] = _token_rows(fbuf, f0 + steps, CONV_BUF)
    c_ref[...] = _ln_silu(_token_rows(cbuf, 0, nseq * steps), gln_ref[...], bln_ref[...])


def _drop_refs(body, first, count, *refs):
    return body(*refs[:first], *refs[first + count:])


def _stacked_call(body, args, in_specs, stacked_outs, prev, **kwargs):
    aliases = {}
    if prev is not None:
        first = len(args)
        in_specs = list(in_specs) + [pl.BlockSpec(memory_space=pl.ANY)] * len(prev)
        aliases = {first + j: out for j, out in enumerate(stacked_outs)}
        args = list(args) + list(prev)
        body = functools.partial(_drop_refs, body, first, len(prev))
    return pl.pallas_call(body, in_specs=in_specs, input_output_aliases=aliases, **kwargs)(*args)


def _conv_sample(state_all, u, wdw, bdw, gln, bln, layer, prev, nseq):
    batch = state_all.shape[1]
    steps = u.shape[0] // (batch * SUBLANES)
    hist_spec = pl.BlockSpec((None, nseq, CONV_BUF, D_MODEL), lambda i: (layer, i, 0, 0))
    return _stacked_call(
        functools.partial(_conv_sample_body, steps),
        [state_all, u, wdw, bdw, gln, bln],
        [hist_spec, _rows(nseq * steps * SUBLANES, LANES), _resident(wdw.shape), _resident(bdw.shape),
         _resident((1, D_MODEL)), _resident((1, D_MODEL))],
        [1], prev,
        grid=(batch // nseq,),
        out_specs=[_rows(nseq * steps, D_MODEL), hist_spec],
        out_shape=[jax.ShapeDtypeStruct((batch * steps, D_MODEL), F32),
                   jax.ShapeDtypeStruct(state_all.shape, F32)],
        scratch_shapes=[pltpu.VMEM((nseq * (CONV_BUF + steps) * SUBLANES, LANES), F32),
                        pltpu.VMEM((nseq * steps * SUBLANES, LANES), F32)],
        compiler_params=_cparams(1),
        name="conv_sample",
    )


def _col(x, h):
    return x[:, h:h + 1]


def _with_ones(vh):
    lane = lax.broadcasted_iota(jnp.int32, (vh.shape[0], LANES), 1)
    return jnp.concatenate([vh, (lane == 0).astype(vh.dtype)], axis=1)


def _intra_head(s, vaug, b_col, r_row, inter_col, mask):
    dm = jnp.where(mask, b_col + r_row, NEG)
    m_t = jnp.maximum(inter_col, jnp.max(dm, axis=1, keepdims=True))
    sw = s * jnp.exp(dm - m_t)
    return _dot(sw.astype(BF16), vaug), m_t, jnp.exp(inter_col - m_t)


def _head_out(num_aug, m_t, gmh, so):
    h = num_aug[:, :DV] * (1.0 / jnp.maximum(jnp.abs(num_aug[:, DV:DV + 1]), jnp.exp(-m_t)))
    scale = lax.rsqrt(jnp.mean(h * h, axis=-1, keepdims=True) + EPS)
    return (h * scale * (gmh * so.astype(F32))).astype(BF16)


def _mlstm_prompt_body(q_ref, kt_ref, v_ref, so_ref, gi_ref, gf_ref, gmh_ref,
                       hmo_ref, c_ref, n_ref, m_ref, ct_ref):
    L = CHUNK
    step = pl.program_id(1)

    @pl.when(step == 0)
    def _():
        ct_ref[...] = jnp.zeros(ct_ref.shape, F32)
        m_ref[...] = jnp.zeros(m_ref.shape, F32)

    row = lax.broadcasted_iota(jnp.int32, (L, LANES), 0)
    tri = lax.broadcasted_iota(jnp.int32, (L, L), 1) <= lax.broadcasted_iota(jnp.int32, (L, L), 0)
    heads = range(M_HEADS)
    m_prev = m_ref[0]
    for t0 in range(0, q_ref.shape[0], L):
        gi = gi_ref[t0:t0 + L, :]
        b = gf_ref[t0:t0 + L, :]
        sh = 1
        while sh < L:
            b = b + jnp.where(row >= sh, pltpu.roll(b, sh, 0), 0.0)
            sh *= 2
        r = gi - b
        r_t = r.T
        b_last = b[L - 1:L, :]
        m_new = jnp.maximum(b_last + m_prev, jnp.max(b_last + r, axis=0, keepdims=True))
        dec = jnp.exp(b_last + m_prev - m_new)
        inter = b + m_prev
        qs = [q_ref[t0:t0 + L, h * DK:(h + 1) * DK] for h in heads]
        kts = [kt_ref[h * DK:(h + 1) * DK, t0:t0 + L] for h in heads]
        vaugs = [_with_ones(v_ref[t0:t0 + L, h * DV:(h + 1) * DV]) for h in heads]
        scores = [_dot(qs[h], kts[h]) for h in heads]
        ct_olds = [ct_ref[h] for h in heads]
        reads = [_dot(qs[h], ct_olds[h].astype(BF16)) for h in heads]
        for h in heads:
            wk_row = jnp.exp(_col(b_last, h) + r_t[h:h + 1, :] - _col(m_new, h))
            ct_ref[h] = _col(dec, h) * ct_olds[h] + _dot((kts[h].astype(F32) * wk_row).astype(BF16), vaugs[h])
        for h in heads:
            num_aug, m_t, w_int = _intra_head(
                scores[h], vaugs[h], _col(b, h), r_t[h:h + 1, :], _col(inter, h), tri)
            hmo_ref[t0:t0 + L, h * DV:(h + 1) * DV] = _head_out(
                num_aug + w_int * reads[h], m_t, gmh_ref[:, h * DV:(h + 1) * DV],
                so_ref[t0:t0 + L, h * DV:(h + 1) * DV])
        m_prev = m_new
    m_ref[0] = m_prev

    @pl.when(step == pl.num_programs(1) - 1)
    def _():
        for h in range(M_HEADS):
            ct = ct_ref[h]
            c_ref[0, h] = ct[:, :DV].T
            n_ref[0, h:h + 1, :] = ct[:, DV:].T[0:1, :]


def _mlstm_prompt(q, kt, v, so, gi, gf, gmh, batch, seq, depth, layer, prev):
    rows = MLSTM_CHUNKS * CHUNK
    nc = seq // rows
    n = batch * seq

    def tok(width):
        return pl.BlockSpec((rows, width), lambda b, c: (b * nc + c, 0))

    return _stacked_call(
        _mlstm_prompt_body,
        [q, kt, v, so, gi, gf, gmh],
        [tok(D_QK), pl.BlockSpec((D_QK, rows), lambda b, c: (0, b * nc + c)),
         tok(D_M), tok(D_M), tok(LANES), tok(LANES), _resident((1, D_M))],
        [1, 2], prev,
        grid=(batch, nc),
        out_specs=[tok(D_M),
                   pl.BlockSpec((None, 1, M_HEADS, DV, DK), lambda b, c: (layer, b, 0, 0, 0)),
                   pl.BlockSpec((None, 1, M_HEADS, DK), lambda b, c: (layer, b, 0, 0)),
                   pl.BlockSpec((1, 1, LANES), lambda b, c: (b, 0, 0))],
        out_shape=[jax.ShapeDtypeStruct((n, D_M), BF16),
                   jax.ShapeDtypeStruct((depth, batch, M_HEADS, DV, DK), F32),
                   jax.ShapeDtypeStruct((depth, batch, M_HEADS, DK), F32),
                   jax.ShapeDtypeStruct((batch, 1, LANES), F32)],
        scratch_shapes=[pltpu.VMEM((M_HEADS, DK, AUG), F32)],
        compiler_params=_cparams(2),
        name="mlstm_prompt",
    )


def _mlstm_sample_intra_body(steps, q_ref, k_ref, v_ref, gi_ref, gf_ref, mtok_ref,
                             num_ref, mt_ref, wi_ref, wk_ref, wkt_ref, dec_ref, mnew_ref):
    L = q_ref.shape[0]
    row = lax.broadcasted_iota(jnp.int32, (L, LANES), 0)
    pos = row & (steps - 1)

    def down(x, k):
        return pltpu.roll(x, k, 0)

    def up(x, k):
        return pltpu.roll(x, L - k, 0)

    gi = gi_ref[...]
    b = gf_ref[...]
    sh = 1
    while sh < steps:
        b = b + jnp.where(pos >= sh, down(b, sh), 0.0)
        sh *= 2
    r = gi - b
    r_t = r.T
    m_prev = mtok_ref[...]
    b_last = b
    for k in range(1, steps):
        b_last = jnp.where(pos == steps - 1 - k, up(b, k), b_last)
    d_last = b_last + r
    d_max = d_last
    sh = 1
    while sh < steps:
        d_max = jnp.maximum(d_max, jnp.where((pos & sh) == 0, up(d_max, sh), down(d_max, sh)))
        sh *= 2
    m_new = jnp.maximum(b_last + m_prev, d_max)
    wk = jnp.exp(d_last - m_new)
    inter = b + m_prev
    ti = lax.broadcasted_iota(jnp.int32, (L, L), 0)
    si = lax.broadcasted_iota(jnp.int32, (L, L), 1)
    mask = (si <= ti) & ((si & -steps) == (ti & -steps))
    lane = lax.broadcasted_iota(jnp.int32, (L, LANES), 1)

    mt_all = jnp.zeros((L, LANES), F32)
    wi_all = jnp.zeros((L, LANES), F32)
    for h in range(M_HEADS):
        s = _dot_nt(q_ref[:, h * DK:(h + 1) * DK], k_ref[:, h * DK:(h + 1) * DK])
        num_aug, m_t, w_int = _intra_head(s, _with_ones(v_ref[:, h * DV:(h + 1) * DV]),
                                          _col(b, h), r_t[h:h + 1, :], _col(inter, h), mask)
        num_ref[:, h * AUG:(h + 1) * AUG] = num_aug
        mt_all = jnp.where(lane == h, m_t, mt_all)
        wi_all = jnp.where(lane == h, w_int, wi_all)
    mt_ref[...] = mt_all
    wi_ref[...] = wi_all
    wk_ref[...] = wk
    wkt_ref[...] = wk.T[0:8, :]
    dec_ref[...] = jnp.exp(b_last + m_prev - m_new)
    mnew_ref[...] = m_new


def _mlstm_sample_intra(q, k, v, gi, gf, mtok, steps):
    n = q.shape[0]
    L = CHUNK
    scal = jax.ShapeDtypeStruct((n, LANES), F32)
    return pl.pallas_call(
        functools.partial(_mlstm_sample_intra_body, steps),
        grid=(n // L,),
        in_specs=[_rows(L, D_QK), _rows(L, D_QK), _rows(L, D_M), _rows(L, LANES), _rows(L, LANES),
                  _rows(L, LANES)],
        out_specs=[_rows(L, M_HEADS * AUG), _rows(L, LANES), _rows(L, LANES), _rows(L, LANES),
                   pl.BlockSpec((8, L), lambda i: (0, i)), _rows(L, LANES), _rows(L, LANES)],
        out_shape=[jax.ShapeDtypeStruct((n, M_HEADS * AUG), F32), scal, scal, scal,
                   jax.ShapeDtypeStruct((8, n), F32), scal, scal],
        compiler_params=_cparams(1),
        name="mlstm_sample_intra",
    )(q, k, v, gi, gf, mtok)


def _mlstm_sample_state_body(steps, q_ref, kwin_ref, vt_ref, so_ref, num_ref, mt_ref, wi_ref,
                             wk_ref, wkt_ref, dec_ref, gmh_ref, c_ref, n_ref,
                             hmo_ref, cn_ref, nn_ref):
    rows = q_ref.shape[0]
    win = kwin_ref.shape[0]
    nseq = rows // steps
    first = (pl.program_id(0) % (win // rows)) * rows
    grp = lax.broadcasted_iota(jnp.int32, (rows, 1), 0) & -steps
    lane = lax.broadcasted_iota(jnp.int32, (1, win), 1)
    col = lax.broadcasted_iota(jnp.int32, (rows, LANES), 1)
    wi = wi_ref[...]
    for h in range(M_HEADS):
        qh = q_ref[:, h * DK:(h + 1) * DK]
        qf = qh.astype(F32)
        kwin = kwin_ref[:, h * DK:(h + 1) * DK]
        k_own = kwin_ref[pl.ds(pl.multiple_of(first, rows), rows), h * DK:(h + 1) * DK].astype(F32)
        vt = vt_ref[h * DV:(h + 1) * DV, :].astype(F32)
        wk_col = _col(wk_ref[...], h)
        wk_row = wkt_ref[h:h + 1, :]
        cq = jnp.zeros((rows, DV), F32)
        nq = jnp.zeros((rows, 1), F32)
        for s in range(nseq):
            c_old = c_ref[s, h]
            n_old = n_ref[s, h:h + 1, :]
            own = grp == s * steps
            cq = jnp.where(own, _dot_nt(qh, c_old.astype(BF16)), cq)
            nq = jnp.where(own, jnp.sum(qf * n_old, axis=1, keepdims=True), nq)
            dec = dec_ref[s * steps:s * steps + 1, h:h + 1]
            lo = first + s * steps
            wrow = jnp.where((lane >= lo) & (lane < lo + steps), wk_row, 0.0)
            cn_ref[s, h] = dec * c_old + _dot((vt * wrow).astype(BF16), kwin)
            nn_ref[s, h:h + 1, :] = dec * n_old + jnp.sum(
                k_own * jnp.where(own, wk_col, 0.0), axis=0, keepdims=True)
        inter_aug = jnp.concatenate([cq, jnp.where(col == 0, nq, 0.0)], axis=1)
        num_aug = num_ref[:, h * AUG:(h + 1) * AUG] + _col(wi, h) * inter_aug
        hmo_ref[:, h * DV:(h + 1) * DV] = _head_out(
            num_aug, _col(mt_ref[...], h), gmh_ref[:, h * DV:(h + 1) * DV], so_ref[:, h * DV:(h + 1) * DV])


def _mlstm_sample_state(q, k, vt, so, num, mt, wi, wk, wkt, dec, gmh, c_all, n_all, layer, prev, steps):
    n = q.shape[0]
    batch = c_all.shape[1]
    rows = SEQ_PER_STEP * steps
    per_win = CHUNK // rows

    def win_rows(width):
        return pl.BlockSpec((CHUNK, width), lambda i: (i // per_win, 0))

    c_spec = pl.BlockSpec((None, SEQ_PER_STEP, M_HEADS, DV, DK), lambda i: (layer, i, 0, 0, 0))
    n_spec = pl.BlockSpec((None, SEQ_PER_STEP, M_HEADS, DK), lambda i: (layer, i, 0, 0))
    return _stacked_call(
        functools.partial(_mlstm_sample_state_body, steps),
        [q, k, vt, so, num, mt, wi, wk, wkt, dec, gmh, c_all, n_all],
        [_rows(rows, D_QK), win_rows(D_QK),
         pl.BlockSpec((D_M, CHUNK), lambda i: (0, i // per_win)),
         _rows(rows, D_M), _rows(rows, M_HEADS * AUG), _rows(rows, LANES),
         _rows(rows, LANES), _rows(rows, LANES),
         pl.BlockSpec((8, CHUNK), lambda i: (0, i // per_win)),
         _rows(rows, LANES), _resident((1, D_M)), c_spec, n_spec],
        [1, 2], prev,
        grid=(batch // SEQ_PER_STEP,),
        out_specs=[_rows(rows, D_M), c_spec, n_spec],
        out_shape=[jax.ShapeDtypeStruct((n, D_M), BF16),
                   jax.ShapeDtypeStruct(c_all.shape, F32),
                   jax.ShapeDtypeStruct(n_all.shape, F32)],
        compiler_params=_cparams(1),
        name="mlstm_sample_state",
    )


def _merge_body(x_ref, c_ref, hmo_ref, gc_ref, gm_ref, wco_ref, wmo_ref, wo_ref, gpm_ref, y_ref):
    y_conv = _dot(c_ref[...].astype(BF16), wco_ref[...])
    y_m = _dot(hmo_ref[...], wmo_ref[...])
    mix = gc_ref[...].astype(F32) * y_conv + gm_ref[...].astype(F32) * y_m
    y_ref[...] = x_ref[...] + _rmsnorm(_dot(mix.astype(BF16), wo_ref[...]), gpm_ref[...])


def _merge(x, c, hmo, gc, gm, wco, wmo, wo, layer, gpm, tm):
    n = x.shape[0]
    return pl.pallas_call(
        _merge_body,
        grid=(n // tm,),
        in_specs=[_rows(tm, D_MODEL), _rows(tm, D_MODEL), _rows(tm, D_M), _rows(tm, D_MODEL),
                  _rows(tm, D_MODEL),
                  _layer_of(wco, layer), _layer_of(wmo, layer), _layer_of(wo, layer),
                  _resident((1, D_MODEL))],
        out_specs=_rows(tm, D_MODEL),
        out_shape=jax.ShapeDtypeStruct((n, D_MODEL), F32),
        compiler_params=_cparams(1),
        name="merge",
    )(x, c, hmo, gc, gm, wco, wmo, wo, gpm)


def _ffn_body(x_ref, gpf_ref, w1_ref, b1_ref, w2_ref, b2_ref, gpo_ref, y_ref):
    x1 = x_ref[...]
    hf = _rmsnorm(x1, gpf_ref[...]).astype(BF16)
    a = jnp.maximum(_dot(hf, w1_ref[...]) + b1_ref[...], 0.0)
    ff = _dot((a * a).astype(BF16), w2_ref[...]) + b2_ref[...]
    y_ref[...] = x1 + _rmsnorm(ff, gpo_ref[...])


def _ffn(x, gpf, w1, b1, w2, b2, layer, gpo, tm):
    n = x.shape[0]
    return pl.pallas_call(
        _ffn_body,
        grid=(n // tm,),
        in_specs=[_rows(tm, D_MODEL), _resident((1, D_MODEL)),
                  _layer_of(w1, layer), _resident((1, D_FF)),
                  _layer_of(w2, layer), _resident((1, D_MODEL)), _resident((1, D_MODEL))],
        out_specs=_rows(tm, D_MODEL),
        out_shape=jax.ShapeDtypeStruct((n, D_MODEL), F32),
        compiler_params=_cparams(1),
        name="ffn",
    )(x, gpf, w1, b1, w2, b2, gpo)


def _row(v):
    return v.reshape(1, -1)


def _prep_weights(w_in, b_in, w_conv_out, w_m_out, w_o, w_ff1, w_ff2):
    pad = HEAD_COLS + TAIL_W - w_in.shape[2]
    return dict(w_main=jnp.pad(w_in.astype(BF16), ((0, 0), (0, 0), (0, pad))),
                b_main=jnp.pad(b_in, ((0, 0), (0, pad)))[:, None, :],
                wco=w_conv_out.astype(BF16), wmo=w_m_out.astype(BF16), wo=w_o.astype(BF16),
                w1=w_ff1.astype(BF16), w2=w_ff2.astype(BF16))


def kernel(x_prompt, x_sample, state_conv, state_C, state_n, state_m, g_pre_mix, w_in, b_in, w_dw, b_dw,
           g_cln, b_cln, w_conv_out, g_mh, w_m_out, w_o, g_post_mix, g_pre_ff, w_ff1, b_ff1, w_ff2, b_ff2,
           g_post_ff):
    bp, tp, _ = x_prompt.shape
    bs, ts, _ = x_sample.shape
    depth = w_in.shape[0]
    assert tp % (MLSTM_CHUNKS * CHUNK) == 0 and tp >= CONV_BUF and CHUNK % (SEQ_PER_STEP * ts) == 0
    assert (bs * ts) % CHUNK == 0 and bs % SEQ_PER_STEP == 0 and ts & (ts - 1) == 0
    tm_p = min(256, bp * tp)
    tm_s = min(256, bs * ts)
    assert (bp * tp) % tm_p == 0 and (bs * ts) % tm_s == 0
    yp = x_prompt.reshape(bp * tp, D_MODEL)
    ys = x_sample.reshape(bs * ts, D_MODEL)
    p = _prep_weights(w_in, b_in, w_conv_out, w_m_out, w_o, w_ff1, w_ff2)
    hist_p, m_p, m_s = [], [], []
    prompt_state = sample_state = sample_hist = None
    for d in range(depth):
        conv_args = (w_dw[d].reshape(CONV_WIDTH * SUBLANES, LANES), b_dw[d].reshape(SUBLANES, LANES),
                     _row(g_cln[d]), _row(b_cln[d]))
        gmh = _row(g_mh[d])
        inproj_args = (_row(g_pre_mix[d]), p["w_main"], p["b_main"], d)
        wk_t = w_in[d, :, K_COL:V_COL].T.astype(BF16)
        bk_t = jnp.broadcast_to(b_in[d, K_COL:V_COL].reshape(D_QK, 1), (D_QK, LANES))

        def tail(x, c, hmo, gc, gm, tm):
            x1 = _merge(x, c, hmo, gc, gm, p["wco"], p["wmo"], p["wo"], d, _row(g_post_mix[d]), tm)
            return _ffn(x1, _row(g_pre_ff[d]), p["w1"], _row(b_ff1[d]), p["w2"], _row(b_ff2[d]), d,
                        _row(g_post_ff[d]), tm)

        c, hist, q, kt, v, so, gc, gm, gi, gf = _inproj_conv(
            yp, *inproj_args, wk_t, bk_t, *conv_args, bp, tp, tm_p)
        hmo, *prompt_state, m_new = _mlstm_prompt(q, kt, v, so, gi, gf, gmh, bp, tp, depth, d, prompt_state)
        yp = tail(yp, c, hmo, gc, gm, 2 * tm_p)
        hist_p.append(hist)
        m_p.append(m_new[:, 0, :M_HEADS])

        u, q, k, v, so, gc, gm, gi, gf = _inproj(ys, *inproj_args, tm_s)
        c, *sample_hist = _conv_sample(state_conv, u, *conv_args, d, sample_hist, min(16, bs))
        mtok = jnp.pad(jnp.repeat(state_m[d], ts, axis=0), ((0, 0), (0, LANES - M_HEADS)))
        num, mt, wi, wk, wkt, dec, mnew = _mlstm_sample_intra(q, k, v, gi, gf, mtok, ts)
        hmo, *sample_state = _mlstm_sample_state(q, k, v.T, so, num, mt, wi, wk, wkt, dec, gmh,
                                                 state_C, state_n, d, sample_state, ts)
        ys = tail(ys, c, hmo, gc, gm, tm_s)
        m_s.append(mnew[ts - 1::ts, :M_HEADS])
    return (yp.reshape(bp, tp, D_MODEL), ys.reshape(bs, ts, D_MODEL),
            jnp.stack(hist_p), prompt_state[0], prompt_state[1], jnp.stack(m_p),
            sample_hist[0], sample_state[0], sample_state[1], jnp.stack(m_s))
```

```python
import functools

import jax
import jax.numpy as jnp
from jax import lax
from jax.experimental import pallas as pl
from jax.experimental.pallas import tpu as pltpu

F32 = jnp.float32
BF16 = jnp.bfloat16

D_MODEL = 1024
M_HEADS = 4
DK = 256
DV = 512
D_QK = M_HEADS * DK
D_M = M_HEADS * DV
D_FF = 4 * D_MODEL
CONV_WIDTH = 31
CONV_BUF = CONV_WIDTH - 1
CHUNK = 128
EPS = 1e-6
NEG = -1e30
LANES = 128
SUBLANES = 8
Q_COL = 2 * D_MODEL
K_COL = Q_COL + D_QK
V_COL = K_COL + D_QK
O_COL = V_COL + D_M
HEAD_COLS = O_COL + D_M
TAIL_W = -(-(2 * M_HEADS + 2 * D_MODEL) // LANES) * LANES
AUG = DV + LANES
HALO = 32
CONV_TOKENS = 8
CONV_ROWS = 16
ANCHORS_PER_SEGMENT = 2
SEQ_PER_STEP = 4
MLSTM_CHUNKS = 1

VMEM_LIMIT = 56 * 1024 * 1024


def _cparams(n_axes):
    return pltpu.CompilerParams(dimension_semantics=("arbitrary",) * n_axes,
                                vmem_limit_bytes=VMEM_LIMIT)


def _resident(shape):
    nd = len(shape)
    return pl.BlockSpec(shape, lambda *_: (0,) * nd, pipeline_mode=pl.Buffered(1))


def _layer_of(w, layer):
    tail = (0,) * (w.ndim - 1)
    return pl.BlockSpec((None,) + w.shape[1:], lambda *_: (layer,) + tail, pipeline_mode=pl.Buffered(1))


def _rows(tm, width):
    return pl.BlockSpec((tm, width), lambda i: (i, 0))


def _rmsnorm(x, g):
    return x * lax.rsqrt(jnp.mean(x * x, axis=-1, keepdims=True) + EPS) * g


def _sigmoid(x):
    return 1.0 / (1.0 + jnp.exp(-x))


def _log_sigmoid(x):
    return jnp.minimum(x, 0.0) - jnp.log(1.0 + jnp.exp(-jnp.abs(x)))


def _dot(a, b):
    return jnp.dot(a, b, preferred_element_type=F32)


def _dot_nt(a, b):
    return lax.dot_general(a, b, (((1,), (1,)), ((), ())), preferred_element_type=F32)


def _project(k_transposed, x_ref, g_ref, w_ref, b_ref, wk_ref, bk_ref,
             q_ref, k_ref, v_ref, so_ref, gc_ref, gm_ref, gi_ref, gf_ref, anchors=None):
    hb = _rmsnorm(x_ref[...], g_ref[...]).astype(BF16)

    def seg(lo, hi):
        return _dot(hb, w_ref[:, lo:hi]) + b_ref[:, lo:hi]

    u = seg(0, D_MODEL) * _sigmoid(seg(D_MODEL, Q_COL))
    q = seg(Q_COL, K_COL)
    q_ref[...] = (q * (DK ** -0.5)).astype(BF16)
    if k_transposed:
        k = _dot_nt(wk_ref[...], hb) + bk_ref[:, 0:1]
    else:
        k = seg(K_COL, V_COL)
    k_ref[...] = k.astype(BF16)
    v = seg(V_COL, O_COL)
    v_ref[...] = v.astype(BF16)
    o = seg(O_COL, HEAD_COLS)
    so_ref[...] = _sigmoid(o).astype(BF16)
    if anchors is not None:
        for res, transposed in ((q, False), (k, k_transposed), (v, False), (o, False)):
            span = res.shape[0] if transposed else res.shape[1]
            for i in range(ANCHORS_PER_SEGMENT):
                at = i * (span // ANCHORS_PER_SEGMENT)
                anchors.append(res[at:at + SUBLANES, 0:LANES] if transposed else res[0:SUBLANES, at:at + LANES])
    t = seg(HEAD_COLS, HEAD_COLS + TAIL_W)
    gi_ref[...] = t[:, :LANES]
    gf_ref[...] = _log_sigmoid(pltpu.roll(t[:, :LANES], LANES - M_HEADS, 1))
    t = pltpu.roll(t, TAIL_W - 2 * M_HEADS, 1)
    gc_ref[...] = _sigmoid(t[:, :D_MODEL]).astype(BF16)
    gm_ref[...] = _sigmoid(t[:, D_MODEL:2 * D_MODEL]).astype(BF16)
    return u


def _zero_of(x):
    bits = pltpu.bitcast(x, jnp.uint32)
    bits = lax.shift_right_logical(lax.shift_right_logical(bits, jnp.uint32(16)), jnp.uint32(16))
    return pltpu.bitcast(bits, F32)[0:1, 0:1]


def _store_tokens(buf, first_token, u):
    for c in range(SUBLANES):
        buf[pl.ds(first_token * SUBLANES + c, u.shape[0], stride=SUBLANES), :] = u[:, c * LANES:(c + 1) * LANES]


def _inproj_body(x_ref, g_ref, w_ref, b_ref, u_ref, *out_refs):
    _store_tokens(u_ref, 0, _project(False, x_ref, g_ref, w_ref, b_ref, None, None, *out_refs))


_PIECES = ((D_QK, BF16), (D_QK, BF16), (D_M, BF16), (D_M, BF16), (D_MODEL, BF16), (D_MODEL, BF16),
           (LANES, F32), (LANES, F32))


def _inproj(x, g, w, b, layer, tm):
    n = x.shape[0]
    return pl.pallas_call(
        _inproj_body,
        grid=(n // tm,),
        in_specs=[_rows(tm, D_MODEL), _resident((1, D_MODEL)), _layer_of(w, layer), _layer_of(b, layer)],
        out_specs=[_rows(tm * SUBLANES, LANES)] + [_rows(tm, wd) for wd, _ in _PIECES],
        out_shape=[jax.ShapeDtypeStruct((n * SUBLANES, LANES), F32)]
        + [jax.ShapeDtypeStruct((n, wd), dt) for wd, dt in _PIECES],
        compiler_params=_cparams(1),
        name="inproj",
    )(x, g, w, b)


def _ln_silu(c, g, b):
    mu = jnp.mean(c, axis=-1, keepdims=True)
    xc = c - mu
    y = xc * lax.rsqrt(jnp.mean(xc * xc, axis=-1, keepdims=True) + EPS) * g + b
    return y * _sigmoid(y)


def _conv_taps(win, w_ref, b_ref, n_out):
    even = [b_ref[...]] * n_out
    odd = [None] * n_out
    for j in range(CONV_WIDTH):
        wj = w_ref[j * SUBLANES:(j + 1) * SUBLANES, :]
        prods = [win[t + j] * wj for t in range(n_out)]
        if j % 2 == 0:
            even = [a + p for a, p in zip(even, prods)]
        else:
            odd = [p if a is None else a + p for a, p in zip(odd, prods)]
    return [a + b for a, b in zip(even, odd)]


def _token_rows(buf, first_token, n):
    return jnp.concatenate(
        [buf[pl.ds(first_token * SUBLANES + c, n, stride=SUBLANES), :] for c in range(SUBLANES)], axis=1)


def _inproj_conv_body(x_ref, g_ref, w_ref, b_ref, wk_ref, bk_ref, wdw_ref, bdw_ref, gln_ref, bln_ref,
                      c_ref, hist_ref, *rest):
    piece_refs, (ubuf, cbuf) = rest[:-2], rest[-2:]
    tm = x_ref.shape[0]
    halo = HALO * SUBLANES

    @pl.when(pl.program_id(1) == 0)
    def _():
        ubuf[0:halo, :] = jnp.zeros((halo, LANES), F32)

    anchors = []
    u = _project(True, x_ref, g_ref, w_ref, b_ref, wk_ref, bk_ref, *piece_refs, anchors=anchors)
    part = tm // len(anchors)
    for i, a in enumerate(anchors):
        _store_tokens(ubuf, HALO + i * part, u[i * part:(i + 1) * part, :] + _zero_of(a))
    for t0 in range(0, tm, CONV_TOKENS):
        win = [ubuf[(t0 + HALO - CONV_BUF + k) * SUBLANES:(t0 + HALO - CONV_BUF + k + 1) * SUBLANES, :]
               for k in range(CONV_TOKENS + CONV_BUF)]
        for t, acc in enumerate(_conv_taps(win, wdw_ref, bdw_ref, CONV_TOKENS)):
            cbuf[(t0 + t) * SUBLANES:(t0 + t + 1) * SUBLANES, :] = acc
    for r0 in range(0, tm, CONV_ROWS):
        c_ref[r0:r0 + CONV_ROWS, :] = _ln_silu(
            _token_rows(cbuf, r0, CONV_ROWS), gln_ref[...], bln_ref[...]).astype(c_ref.dtype)

    @pl.when(pl.program_id(1) == pl.num_programs(1) - 1)
    def _():
        hist_ref[0] = _token_rows(ubuf, HALO + tm - CONV_BUF, CONV_BUF)

    ubuf[0:halo, :] = ubuf[tm * SUBLANES:tm * SUBLANES + halo, :]


def _inproj_conv(x, g, w, b, layer, wk_t, bk_t, wdw, bdw, gln, bln, batch, seq, tm):
    nt = seq // tm
    n = batch * seq

    def tile(width):
        return pl.BlockSpec((tm, width), lambda s, i: (s * nt + i, 0))

    pieces = list(_PIECES)
    piece_specs = [tile(wd) for wd, _ in pieces]
    piece_shapes = [jax.ShapeDtypeStruct((n, wd), dt) for wd, dt in pieces]
    piece_specs[1] = pl.BlockSpec((D_QK, tm), lambda s, i: (0, s * nt + i))
    piece_shapes[1] = jax.ShapeDtypeStruct((D_QK, n), BF16)
    return pl.pallas_call(
        _inproj_conv_body,
        grid=(batch, nt),
        in_specs=[tile(D_MODEL), _resident((1, D_MODEL)), _layer_of(w, layer), _layer_of(b, layer),
                  _resident(wk_t.shape), _resident(bk_t.shape), _resident(wdw.shape), _resident(bdw.shape),
                  _resident((1, D_MODEL)), _resident((1, D_MODEL))],
        out_specs=[tile(D_MODEL), pl.BlockSpec((1, CONV_BUF, D_MODEL), lambda s, i: (s, 0, 0))] + piece_specs,
        out_shape=[jax.ShapeDtypeStruct((n, D_MODEL), BF16),
                   jax.ShapeDtypeStruct((batch, CONV_BUF, D_MODEL), F32)] + piece_shapes,
        scratch_shapes=[pltpu.VMEM(((HALO + tm) * SUBLANES, LANES), F32),
                        pltpu.VMEM((tm * SUBLANES, LANES), F32)],
        compiler_params=_cparams(2),
        name="inproj_conv",
    )(x, g, w, b, wk_t, bk_t, wdw, bdw, gln, bln)


def _conv_sample_body(steps, st_ref, u_ref, w_ref, b_ref, gln_ref, bln_ref, c_ref, hist_ref, fbuf, cbuf):
    nseq = st_ref.shape[0]
    full = CONV_BUF + steps
    for s in range(nseq):
        f0 = s * full
        for c in range(SUBLANES):
            fbuf[pl.ds(f0 * SUBLANES + c, CONV_BUF, stride=SUBLANES), :] = st_ref[s, :, c * LANES:(c + 1) * LANES]
        fbuf[(f0 + CONV_BUF) * SUBLANES:(f0 + full) * SUBLANES, :] = (
            u_ref[s * steps * SUBLANES:(s + 1) * steps * SUBLANES, :])
        win = [fbuf[(f0 + k) * SUBLANES:(f0 + k + 1) * SUBLANES, :] for k in range(full)]
        for t, acc in enumerate(_conv_taps(win, w_ref, b_ref, steps)):
            cbuf[(s * steps + t) * SUBLANES:(s * steps + t + 1) * SUBLANES, :] = acc
        hist_ref[s] = _token_rows(fbuf, f0 + steps, CONV_BUF)
    c_ref[...] = _ln_silu(_token_rows(cbuf, 0, nseq * steps), gln_ref[...], bln_ref[...])


def _drop_refs(body, first, count, *refs):
    return body(*refs[:first], *refs[first + count:])


def _stacked_call(body, args, in_specs, stacked_outs, prev, **kwargs):
    aliases = {}
    if prev is not None:
        first = len(args)
        in_specs = list(in_specs) + [pl.BlockSpec(memory_space=pl.ANY)] * len(prev)
        aliases = {first + j: out for j, out in enumerate(stacked_outs)}
        args = list(args) + list(prev)
        body = functools.partial(_drop_refs, body, first, len(prev))
    return pl.pallas_call(body, in_specs=in_specs, input_output_aliases=aliases, **kwargs)(*args)


def _conv_sample(state_all, u, wdw, bdw, gln, bln, layer, prev, nseq):
    batch = state_all.shape[1]
    steps = u.shape[0] // (batch * SUBLANES)
    hist_spec = pl.BlockSpec((None, nseq, CONV_BUF, D_MODEL), lambda i: (layer, i, 0, 0))
    return _stacked_call(
        functools.partial(_conv_sample_body, steps),
        [state_all, u, wdw, bdw, gln, bln],
        [hist_spec, _rows(nseq * steps * SUBLANES, LANES), _resident(wdw.shape), _resident(bdw.shape),
         _resident((1, D_MODEL)), _resident((1, D_MODEL))],
        [1], prev,
        grid=(batch // nseq,),
        out_specs=[_rows(nseq * steps, D_MODEL), hist_spec],
        out_shape=[jax.ShapeDtypeStruct((batch * steps, D_MODEL), F32),
                   jax.ShapeDtypeStruct(state_all.shape, F32)],
        scratch_shapes=[pltpu.VMEM((nseq * (CONV_BUF + steps) * SUBLANES, LANES), F32),
                        pltpu.VMEM((nseq * steps * SUBLANES, LANES), F32)],
        compiler_params=_cparams(1),
        name="conv_sample",
    )


def _col(x, h):
    return x[:, h:h + 1]


def _with_ones(vh):
    lane = lax.broadcasted_iota(jnp.int32, (vh.shape[0], LANES), 1)
    return jnp.concatenate([vh, (lane == 0).astype(vh.dtype)], axis=1)


def _intra_head(s, vaug, b_col, r_row, inter_col, mask):
    dm = jnp.where(mask, b_col + r_row, NEG)
    m_t = jnp.maximum(inter_col, jnp.max(dm, axis=1, keepdims=True))
    sw = s * jnp.exp(dm - m_t)
    return _dot(sw.astype(BF16), vaug), m_t, jnp.exp(inter_col - m_t)


def _head_out(num_aug, m_t, gmh, so):
    h = num_aug[:, :DV] * (1.0 / jnp.maximum(jnp.abs(num_aug[:, DV:DV + 1]), jnp.exp(-m_t)))
    scale = lax.rsqrt(jnp.mean(h * h, axis=-1, keepdims=True) + EPS)
    return (h * scale * (gmh * so.astype(F32))).astype(BF16)


def _mlstm_prompt_body(q_ref, kt_ref, v_ref, so_ref, gi_ref, gf_ref, gmh_ref,
                       hmo_ref, c_ref, n_ref, m_ref, ct_ref):
    L = CHUNK
    step = pl.program_id(1)

    @pl.when(step == 0)
    def _():
        ct_ref[...] = jnp.zeros(ct_ref.shape, F32)
        m_ref[...] = jnp.zeros(m_ref.shape, F32)

    row = lax.broadcasted_iota(jnp.int32, (L, LANES), 0)
    tri = lax.broadcasted_iota(jnp.int32, (L, L), 1) <= lax.broadcasted_iota(jnp.int32, (L, L), 0)
    heads = range(M_HEADS)
    m_prev = m_ref[0]
    for t0 in range(0, q_ref.shape[0], L):
        gi = gi_ref[t0:t0 + L, :]
        b = gf_ref[t0:t0 + L, :]
        sh = 1
        while sh < L:
            b = b + jnp.where(row >= sh, pltpu.roll(b, sh, 0), 0.0)
            sh *= 2
        r = gi - b
        r_t = r.T
        b_last = b[L - 1:L, :]
        m_new = jnp.maximum(b_last + m_prev, jnp.max(b_last + r, axis=0, keepdims=True))
        dec = jnp.exp(b_last + m_prev - m_new)
        inter = b + m_prev
        qs = [q_ref[t0:t0 + L, h * DK:(h + 1) * DK] for h in heads]
        kts = [kt_ref[h * DK:(h + 1) * DK, t0:t0 + L] for h in heads]
        vaugs = [_with_ones(v_ref[t0:t0 + L, h * DV:(h + 1) * DV]) for h in heads]
        scores = [_dot(qs[h], kts[h]) for h in heads]
        ct_olds = [ct_ref[h] for h in heads]
        half = DV // 2
        cols = (slice(0, half), slice(half, AUG))
        ct_bfs = [ct_olds[h].astype(BF16) for h in heads]
        reads = [[_dot(qs[h], ct_bfs[h][:, c]) for c in cols] for h in heads]
        for h in heads:
            wk_row = jnp.exp(_col(b_last, h) + r_t[h:h + 1, :] - _col(m_new, h))
            ct_ref[h] = _col(dec, h) * ct_olds[h] + _dot((kts[h].astype(F32) * wk_row).astype(BF16), vaugs[h])
        for h in heads:
            dm = jnp.where(tri, _col(b, h) + r_t[h:h + 1, :], NEG)
            m_t = jnp.maximum(_col(inter, h), jnp.max(dm, axis=1, keepdims=True))
            sw = (scores[h] * jnp.exp(dm - m_t)).astype(BF16)
            w_int = jnp.exp(_col(inter, h) - m_t)
            lo, hi = [_dot(sw, vaugs[h][:, c]) + w_int * reads[h][i] for i, c in enumerate(cols)]
            inv = 1.0 / jnp.maximum(jnp.abs(hi[:, half:half + 1]), jnp.exp(-m_t))
            h_lo, h_hi = lo * inv, hi[:, :half] * inv
            msq = (jnp.sum(h_lo * h_lo, axis=-1, keepdims=True)
                   + jnp.sum(h_hi * h_hi, axis=-1, keepdims=True)) * (1.0 / DV)
            scale = lax.rsqrt(msq + EPS)
            for part, c0 in ((h_lo, h * DV), (h_hi, h * DV + half)):
                hmo_ref[t0:t0 + L, c0:c0 + half] = (
                    part * scale * (gmh_ref[:, c0:c0 + half] * so_ref[t0:t0 + L, c0:c0 + half].astype(F32))
                ).astype(BF16)
        m_prev = m_new
    m_ref[0] = m_prev

    @pl.when(step == pl.num_programs(1) - 1)
    def _():
        for h in range(M_HEADS):
            ct = ct_ref[h]
            c_ref[0, h] = ct[:, :DV].T
            n_ref[0, h:h + 1, :] = ct[:, DV:].T[0:1, :]


def _mlstm_prompt(q, kt, v, so, gi, gf, gmh, batch, seq, depth, layer, prev):
    rows = MLSTM_CHUNKS * CHUNK
    nc = seq // rows
    n = batch * seq

    def tok(width):
        return pl.BlockSpec((rows, width), lambda b, c: (b * nc + c, 0))

    return _stacked_call(
        _mlstm_prompt_body,
        [q, kt, v, so, gi, gf, gmh],
        [tok(D_QK), pl.BlockSpec((D_QK, rows), lambda b, c: (0, b * nc + c)),
         tok(D_M), tok(D_M), tok(LANES), tok(LANES), _resident((1, D_M))],
        [1, 2], prev,
        grid=(batch, nc),
        out_specs=[tok(D_M),
                   pl.BlockSpec((None, 1, M_HEADS, DV, DK), lambda b, c: (layer, b, 0, 0, 0)),
                   pl.BlockSpec((None, 1, M_HEADS, DK), lambda b, c: (layer, b, 0, 0)),
                   pl.BlockSpec((1, 1, LANES), lambda b, c: (b, 0, 0))],
        out_shape=[jax.ShapeDtypeStruct((n, D_M), BF16),
                   jax.ShapeDtypeStruct((depth, batch, M_HEADS, DV, DK), F32),
                   jax.ShapeDtypeStruct((depth, batch, M_HEADS, DK), F32),
                   jax.ShapeDtypeStruct((batch, 1, LANES), F32)],
        scratch_shapes=[pltpu.VMEM((M_HEADS, DK, AUG), F32)],
        compiler_params=_cparams(2),
        name="mlstm_prompt",
    )


def _mlstm_sample_intra_body(steps, q_ref, k_ref, v_ref, gi_ref, gf_ref, mtok_ref,
                             num_ref, mt_ref, wi_ref, wk_ref, wkt_ref, dec_ref, mnew_ref):
    L = q_ref.shape[0]
    row = lax.broadcasted_iota(jnp.int32, (L, LANES), 0)
    pos = row & (steps - 1)

    def down(x, k):
        return pltpu.roll(x, k, 0)

    def up(x, k):
        return pltpu.roll(x, L - k, 0)

    gi = gi_ref[...]
    b = gf_ref[...]
    sh = 1
    while sh < steps:
        b = b + jnp.where(pos >= sh, down(b, sh), 0.0)
        sh *= 2
    r = gi - b
    r_t = r.T
    m_prev = mtok_ref[...]
    b_last = b
    for k in range(1, steps):
        b_last = jnp.where(pos == steps - 1 - k, up(b, k), b_last)
    d_last = b_last + r
    d_max = d_last
    sh = 1
    while sh < steps:
        d_max = jnp.maximum(d_max, jnp.where((pos & sh) == 0, up(d_max, sh), down(d_max, sh)))
        sh *= 2
    m_new = jnp.maximum(b_last + m_prev, d_max)
    wk = jnp.exp(d_last - m_new)
    inter = b + m_prev
    ti = lax.broadcasted_iota(jnp.int32, (L, L), 0)
    si = lax.broadcasted_iota(jnp.int32, (L, L), 1)
    mask = (si <= ti) & ((si & -steps) == (ti & -steps))
    lane = lax.broadcasted_iota(jnp.int32, (L, LANES), 1)

    mt_all = jnp.zeros((L, LANES), F32)
    wi_all = jnp.zeros((L, LANES), F32)
    for h in range(M_HEADS):
        s = _dot_nt(q_ref[:, h * DK:(h + 1) * DK], k_ref[:, h * DK:(h + 1) * DK])
        num_aug, m_t, w_int = _intra_head(s, _with_ones(v_ref[:, h * DV:(h + 1) * DV]),
                                          _col(b, h), r_t[h:h + 1, :], _col(inter, h), mask)
        num_ref[:, h * AUG:(h + 1) * AUG] = num_aug
        mt_all = jnp.where(lane == h, m_t, mt_all)
        wi_all = jnp.where(lane == h, w_int, wi_all)
    mt_ref[...] = mt_all
    wi_ref[...] = wi_all
    wk_ref[...] = wk
    wkt_ref[...] = wk.T[0:8, :]
    dec_ref[...] = jnp.exp(b_last + m_prev - m_new)
    mnew_ref[...] = m_new


def _mlstm_sample_intra(q, k, v, gi, gf, mtok, steps):
    n = q.shape[0]
    L = CHUNK
    scal = jax.ShapeDtypeStruct((n, LANES), F32)
    return pl.pallas_call(
        functools.partial(_mlstm_sample_intra_body, steps),
        grid=(n // L,),
        in_specs=[_rows(L, D_QK), _rows(L, D_QK), _rows(L, D_M), _rows(L, LANES), _rows(L, LANES),
                  _rows(L, LANES)],
        out_specs=[_rows(L, M_HEADS * AUG), _rows(L, LANES), _rows(L, LANES), _rows(L, LANES),
                   pl.BlockSpec((8, L), lambda i: (0, i)), _rows(L, LANES), _rows(L, LANES)],
        out_shape=[jax.ShapeDtypeStruct((n, M_HEADS * AUG), F32), scal, scal, scal,
                   jax.ShapeDtypeStruct((8, n), F32), scal, scal],
        compiler_params=_cparams(1),
        name="mlstm_sample_intra",
    )(q, k, v, gi, gf, mtok)


def _mlstm_sample_state_body(steps, q_ref, kwin_ref, vt_ref, so_ref, num_ref, mt_ref, wi_ref,
                             wk_ref, wkt_ref, dec_ref, gmh_ref, c_ref, n_ref,
                             hmo_ref, cn_ref, nn_ref):
    rows = q_ref.shape[0]
    win = kwin_ref.shape[0]
    nseq = rows // steps
    first = (pl.program_id(0) % (win // rows)) * rows
    grp = lax.broadcasted_iota(jnp.int32, (rows, 1), 0) & -steps
    lane = lax.broadcasted_iota(jnp.int32, (1, win), 1)
    col = lax.broadcasted_iota(jnp.int32, (rows, LANES), 1)
    wi = wi_ref[...]
    for h in range(M_HEADS):
        qh = q_ref[:, h * DK:(h + 1) * DK]
        qf = qh.astype(F32)
        kwin = kwin_ref[:, h * DK:(h + 1) * DK]
        k_own = kwin_ref[pl.ds(pl.multiple_of(first, rows), rows), h * DK:(h + 1) * DK].astype(F32)
        vt = vt_ref[h * DV:(h + 1) * DV, :].astype(F32)
        wk_col = _col(wk_ref[...], h)
        wk_row = wkt_ref[h:h + 1, :]
        cq = jnp.zeros((rows, DV), F32)
        nq = jnp.zeros((rows, 1), F32)
        for s in range(nseq):
            c_old = c_ref[s, h]
            n_old = n_ref[s, h:h + 1, :]
            own = grp == s * steps
            cq = jnp.where(own, _dot_nt(qh, c_old.astype(BF16)), cq)
            nq = jnp.where(own, jnp.sum(qf * n_old, axis=1, keepdims=True), nq)
            dec = dec_ref[s * steps:s * steps + 1, h:h + 1]
            lo = first + s * steps
            wrow = jnp.where((lane >= lo) & (lane < lo + steps), wk_row, 0.0)
            cn_ref[s, h] = dec * c_old + _dot((vt * wrow).astype(BF16), kwin)
            nn_ref[s, h:h + 1, :] = dec * n_old + jnp.sum(
                k_own * jnp.where(own, wk_col, 0.0), axis=0, keepdims=True)
        inter_aug = jnp.concatenate([cq, jnp.where(col == 0, nq, 0.0)], axis=1)
        num_aug = num_ref[:, h * AUG:(h + 1) * AUG] + _col(wi, h) * inter_aug
        hmo_ref[:, h * DV:(h + 1) * DV] = _head_out(
            num_aug, _col(mt_ref[...], h), gmh_ref[:, h * DV:(h + 1) * DV], so_ref[:, h * DV:(h + 1) * DV])


def _mlstm_sample_state(q, k, vt, so, num, mt, wi, wk, wkt, dec, gmh, c_all, n_all, layer, prev, steps):
    n = q.shape[0]
    batch = c_all.shape[1]
    rows = SEQ_PER_STEP * steps
    per_win = CHUNK // rows

    def win_rows(width):
        return pl.BlockSpec((CHUNK, width), lambda i: (i // per_win, 0))

    c_spec = pl.BlockSpec((None, SEQ_PER_STEP, M_HEADS, DV, DK), lambda i: (layer, i, 0, 0, 0))
    n_spec = pl.BlockSpec((None, SEQ_PER_STEP, M_HEADS, DK), lambda i: (layer, i, 0, 0))
    return _stacked_call(
        functools.partial(_mlstm_sample_state_body, steps),
        [q, k, vt, so, num, mt, wi, wk, wkt, dec, gmh, c_all, n_all],
        [_rows(rows, D_QK), win_rows(D_QK),
         pl.BlockSpec((D_M, CHUNK), lambda i: (0, i // per_win)),
         _rows(rows, D_M), _rows(rows, M_HEADS * AUG), _rows(rows, LANES),
         _rows(rows, LANES), _rows(rows, LANES),
         pl.BlockSpec((8, CHUNK), lambda i: (0, i // per_win)),
         _rows(rows, LANES), _resident((1, D_M)), c_spec, n_spec],
        [1, 2], prev,
        grid=(batch // SEQ_PER_STEP,),
        out_specs=[_rows(rows, D_M), c_spec, n_spec],
        out_shape=[jax.ShapeDtypeStruct((n, D_M), BF16),
                   jax.ShapeDtypeStruct(c_all.shape, F32),
                   jax.ShapeDtypeStruct(n_all.shape, F32)],
        compiler_params=_cparams(1),
        name="mlstm_sample_state",
    )


def _merge_body(x_ref, c_ref, hmo_ref, gc_ref, gm_ref, wco_ref, wmo_ref, wo_ref, gpm_ref, y_ref):
    y_conv = _dot(c_ref[...].astype(BF16), wco_ref[...])
    y_m = _dot(hmo_ref[...], wmo_ref[...])
    mix = gc_ref[...].astype(F32) * y_conv + gm_ref[...].astype(F32) * y_m
    y_ref[...] = x_ref[...] + _rmsnorm(_dot(mix.astype(BF16), wo_ref[...]), gpm_ref[...])


def _merge(x, c, hmo, gc, gm, wco, wmo, wo, layer, gpm, tm):
    n = x.shape[0]
    return pl.pallas_call(
        _merge_body,
        grid=(n // tm,),
        in_specs=[_rows(tm, D_MODEL), _rows(tm, D_MODEL), _rows(tm, D_M), _rows(tm, D_MODEL),
                  _rows(tm, D_MODEL),
                  _layer_of(wco, layer), _layer_of(wmo, layer), _layer_of(wo, layer),
                  _resident((1, D_MODEL))],
        out_specs=_rows(tm, D_MODEL),
        out_shape=jax.ShapeDtypeStruct((n, D_MODEL), F32),
        compiler_params=_cparams(1),
        name="merge",
    )(x, c, hmo, gc, gm, wco, wmo, wo, gpm)


def _ffn_body(x_ref, gpf_ref, w1_ref, b1_ref, w2_ref, b2_ref, gpo_ref, y_ref):
    x1 = x_ref[...]
    hf = _rmsnorm(x1, gpf_ref[...]).astype(BF16)
    a = jnp.maximum(_dot(hf, w1_ref[...]) + b1_ref[...], 0.0)
    ff = _dot((a * a).astype(BF16), w2_ref[...]) + b2_ref[...]
    y_ref[...] = x1 + _rmsnorm(ff, gpo_ref[...])


def _ffn(x, gpf, w1, b1, w2, b2, layer, gpo, tm):
    n = x.shape[0]
    return pl.pallas_call(
        _ffn_body,
        grid=(n // tm,),
        in_specs=[_rows(tm, D_MODEL), _resident((1, D_MODEL)),
                  _layer_of(w1, layer), _resident((1, D_FF)),
                  _layer_of(w2, layer), _resident((1, D_MODEL)), _resident((1, D_MODEL))],
        out_specs=_rows(tm, D_MODEL),
        out_shape=jax.ShapeDtypeStruct((n, D_MODEL), F32),
        compiler_params=_cparams(1),
        name="ffn",
    )(x, gpf, w1, b1, w2, b2, gpo)


def _row(v):
    return v.reshape(1, -1)


def _prep_weights(w_in, b_in, w_conv_out, w_m_out, w_o, w_ff1, w_ff2):
    pad = HEAD_COLS + TAIL_W - w_in.shape[2]
    return dict(w_main=jnp.pad(w_in.astype(BF16), ((0, 0), (0, 0), (0, pad))),
                b_main=jnp.pad(b_in, ((0, 0), (0, pad)))[:, None, :],
                wco=w_conv_out.astype(BF16), wmo=w_m_out.astype(BF16), wo=w_o.astype(BF16),
                w1=w_ff1.astype(BF16), w2=w_ff2.astype(BF16))


def kernel(x_prompt, x_sample, state_conv, state_C, state_n, state_m, g_pre_mix, w_in, b_in, w_dw, b_dw,
           g_cln, b_cln, w_conv_out, g_mh, w_m_out, w_o, g_post_mix, g_pre_ff, w_ff1, b_ff1, w_ff2, b_ff2,
           g_post_ff):
    bp, tp, _ = x_prompt.shape
    bs, ts, _ = x_sample.shape
    depth = w_in.shape[0]
    assert tp % (MLSTM_CHUNKS * CHUNK) == 0 and tp >= CONV_BUF and CHUNK % (SEQ_PER_STEP * ts) == 0
    assert (bs * ts) % CHUNK == 0 and bs % SEQ_PER_STEP == 0 and ts & (ts - 1) == 0
    tm_p = min(256, bp * tp)
    tm_s = min(256, bs * ts)
    assert (bp * tp) % tm_p == 0 and (bs * ts) % tm_s == 0
    yp = x_prompt.reshape(bp * tp, D_MODEL)
    ys = x_sample.reshape(bs * ts, D_MODEL)
    p = _prep_weights(w_in, b_in, w_conv_out, w_m_out, w_o, w_ff1, w_ff2)
    hist_p, m_p, m_s = [], [], []
    prompt_state = sample_state = sample_hist = None
    for d in range(depth):
        conv_args = (w_dw[d].reshape(CONV_WIDTH * SUBLANES, LANES), b_dw[d].reshape(SUBLANES, LANES),
                     _row(g_cln[d]), _row(b_cln[d]))
        gmh = _row(g_mh[d])
        inproj_args = (_row(g_pre_mix[d]), p["w_main"], p["b_main"], d)
        wk_t = w_in[d, :, K_COL:V_COL].T.astype(BF16)
        bk_t = jnp.broadcast_to(b_in[d, K_COL:V_COL].reshape(D_QK, 1), (D_QK, LANES))

        def tail(x, c, hmo, gc, gm, tm):
            x1 = _merge(x, c, hmo, gc, gm, p["wco"], p["wmo"], p["wo"], d, _row(g_post_mix[d]), tm)
            return _ffn(x1, _row(g_pre_ff[d]), p["w1"], _row(b_ff1[d]), p["w2"], _row(b_ff2[d]), d,
                        _row(g_post_ff[d]), tm)

        c, hist, q, kt, v, so, gc, gm, gi, gf = _inproj_conv(
            yp, *inproj_args, wk_t, bk_t, *conv_args, bp, tp, tm_p)
        hmo, *prompt_state, m_new = _mlstm_prompt(q, kt, v, so, gi, gf, gmh, bp, tp, depth, d, prompt_state)
        yp = tail(yp, c, hmo, gc, gm, 2 * tm_p)
        hist_p.append(hist)
        m_p.append(m_new[:, 0, :M_HEADS])

        u, q, k, v, so, gc, gm, gi, gf = _inproj(ys, *inproj_args, tm_s)
        c, *sample_hist = _conv_sample(state_conv, u, *conv_args, d, sample_hist, min(16, bs))
        mtok = jnp.pad(jnp.repeat(state_m[d], ts, axis=0), ((0, 0), (0, LANES - M_HEADS)))
        num, mt, wi, wk, wkt, dec, mnew = _mlstm_sample_intra(q, k, v, gi, gf, mtok, ts)
        hmo, *sample_state = _mlstm_sample_state(q, k, v.T, so, num, mt, wi, wk, wkt, dec, gmh,
                                                 state_C, state_n, d, sample_state, ts)
        ys = tail(ys, c, hmo, gc, gm, tm_s)
        m_s.append(mnew[ts - 1::ts, :M_HEADS])
    return (yp.reshape(bp, tp, D_MODEL), ys.reshape(bs, ts, D_MODEL),
            jnp.stack(hist_p), prompt_state[0], prompt_state[1], jnp.stack(m_p),
            sample_hist[0], sample_state[0], sample_state[1], jnp.stack(m_s))
```
